```python
import jax, jax.numpy as jnp
from jax import lax
import numpy as np

D_MODEL = 1024
BATCH = 4
SEQ = 4096
DEPTH = 2

N_A = DEPTH // 2
N_B = DEPTH - N_A

LRU_WIDTH = D_MODEL
LRU_HEADS = 8
LRU_BLOCK = LRU_WIDTH // LRU_HEADS
LRU_CONV = 4
LRU_C = 8.0

N_HEADS = 8
HEAD_DIM = D_MODEL // N_HEADS
Q_BLOCK = 128

D_FF = 3 * D_MODEL
FFN_CONV = 3

DN_ALPHA = (2 * DEPTH) ** 0.25
DN_BETA = (8 * DEPTH) ** -0.25
LN_EPS = 1e-5

kernel_name = "yoco_rglru_stickbreak_convffn_deepnorm"


def layer_norm(x, g, b):
    xf = x.astype(jnp.float32)
    mu = jnp.mean(xf, axis=-1, keepdims=True)
    xc = xf - mu
    var = jnp.mean(xc * xc, axis=-1, keepdims=True)
    y = xc * lax.rsqrt(var + LN_EPS) * g.astype(jnp.float32) + b.astype(jnp.float32)
    return y.astype(x.dtype)


def causal_depthwise_conv(x, w, b):
    K = w.shape[0]
    S = x.shape[1]
    xp = jnp.pad(x, ((0, 0), (K - 1, 0), (0, 0)))
    y = b + xp[:, 0:S] * w[0]
    for k in range(1, K):
        y = y + xp[:, k:k + S] * w[k]
    return y


def _lin_combine(left, right):
    a1, b1 = left
    a2, b2 = right
    return a1 * a2, a2 * b1 + b2


def rg_lru_block(x, w_in, b_in, conv_w, conv_b, w_gates, b_gates, a_param, w_out, b_out):
    B, S, _ = x.shape
    proj = x @ w_in + b_in
    y_br = jax.nn.gelu(proj[..., :LRU_WIDTH])
    x_br = causal_depthwise_conv(proj[..., LRU_WIDTH:], conv_w, conv_b)
    xb = x_br.reshape(B, S, LRU_HEADS, LRU_BLOCK)
    gates = jnp.einsum('bsnc,ncg->bsng', xb, w_gates) + b_gates
    gates = jax.nn.sigmoid(gates.astype(jnp.float32))
    gate_i = gates[..., :LRU_BLOCK].reshape(B, S, LRU_WIDTH)
    gate_r = gates[..., LRU_BLOCK:].reshape(B, S, LRU_WIDTH)
    log_a = -LRU_C * gate_r * jax.nn.softplus(-a_param.astype(jnp.float32))
    a = jnp.exp(log_a)
    mult = jnp.sqrt(jnp.maximum(1.0 - jnp.exp(2.0 * log_a), 0.0))
    is_start = (jnp.arange(S) == 0)[None, :, None]
    mult = jnp.where(is_start, 1.0, mult)
    u = mult * gate_i * x_br.astype(jnp.float32)
    _, h = lax.associative_scan(_lin_combine, (a, u), axis=1)
    return (h.astype(x.dtype) * y_br) @ w_out + b_out


def stick_breaking_attention(q, k, v):
    B, S, H, Dh = q.shape
    scale = Dh ** -0.5
    outs = []
    for i in range(S // Q_BLOCK):
        q0 = i * Q_BLOCK
        kv_len = q0 + Q_BLOCK
        q_blk = q[:, q0:kv_len]
        k_p = k[:, :kv_len]
        v_p = v[:, :kv_len]
        z = jnp.einsum('bqhd,bkhd->bhqk', q_blk, k_p).astype(jnp.float32) * scale
        q_pos = q0 + jnp.arange(Q_BLOCK)
        k_pos = jnp.arange(kv_len)
        mask = k_pos[None, :] < q_pos[:, None]
        log_beta = jax.nn.log_sigmoid(z)
        log_1m = jnp.where(mask, jax.nn.log_sigmoid(-z), 0.0)
        suffix = lax.cumsum(log_1m, axis=3, reverse=True) - log_1m
        w = jnp.where(mask, jnp.exp(log_beta + suffix), 0.0)
        o = jnp.einsum('bhqk,bkhd->bqhd', w, v_p.astype(jnp.float32))
        outs.append(o.astype(q.dtype))
    return jnp.concatenate(outs, axis=1)


def conv_ffn(x, w_up, conv_w, conv_b, w_down):
    h = causal_depthwise_conv(x @ w_up, conv_w, conv_b)
    return (jax.nn.gelu(h[..., :D_FF]) * h[..., D_FF:]) @ w_down


def setup_inputs(seed: int = 0) -> dict:
    key = jax.random.key(seed)
    ks = jax.random.split(key, 24)
    f32 = jnp.float32

    def nrm(k, shape, fan_in, gain=1.0):
        return jax.random.normal(k, shape, f32) * (gain * fan_in ** -0.5)

    def small(k, shape):
        return 0.01 * jax.random.normal(k, shape, f32)

    W, HD = LRU_WIDTH, N_HEADS * HEAD_DIM
    u = jax.random.uniform(ks[8], (N_A, W), f32, minval=0.9, maxval=0.999)
    s = u ** (1.0 / LRU_C)
    lru_a_param = jnp.log(s) - jnp.log1p(-s)
    kv_w = jnp.concatenate([nrm(ks[11], (D_MODEL, HD), D_MODEL),
                            nrm(ks[12], (D_MODEL, HD), D_MODEL, DN_BETA)], axis=1)
    return {
        "x": jax.random.normal(ks[0], (BATCH, SEQ, D_MODEL), f32),
        "lru_w_in": nrm(ks[1], (N_A, D_MODEL, 2 * W), D_MODEL),
        "lru_b_in": small(ks[2], (N_A, 2 * W)),
        "lru_conv_w": nrm(ks[3], (N_A, LRU_CONV, W), LRU_CONV),
        "lru_conv_b": small(ks[4], (N_A, W)),
        "lru_w_gates": nrm(ks[5], (N_A, LRU_HEADS, LRU_BLOCK, 2 * LRU_BLOCK), LRU_BLOCK),
        "lru_b_gates": small(ks[6], (N_A, LRU_HEADS, 2 * LRU_BLOCK)),
        "lru_a_param": lru_a_param,
        "lru_w_out": nrm(ks[9], (N_A, W, D_MODEL), W, DN_BETA),
        "lru_b_out": small(ks[10], (N_A, D_MODEL)),
        "kv_w": kv_w,
        "attn_w_q": nrm(ks[13], (N_B, D_MODEL, HD), D_MODEL),
        "attn_w_out": nrm(ks[14], (N_B, HD, D_MODEL), HD, DN_BETA),
        "ffn_w_up": nrm(ks[15], (DEPTH, D_MODEL, 2 * D_FF), D_MODEL),
        "ffn_conv_w": nrm(ks[16], (DEPTH, FFN_CONV, 2 * D_FF), FFN_CONV),
        "ffn_conv_b": small(ks[17], (DEPTH, 2 * D_FF)),
        "ffn_w_down": nrm(ks[18], (DEPTH, D_FF, D_MODEL), D_FF, DN_BETA),
        "ln_g": 1.0 + small(ks[19], (DEPTH, 2, D_MODEL)),
        "ln_b": small(ks[20], (DEPTH, 2, D_MODEL)),
    }


def reference(x, lru_w_in, lru_b_in, lru_conv_w, lru_conv_b, lru_w_gates, lru_b_gates,
              lru_a_param, lru_w_out, lru_b_out, kv_w, attn_w_q, attn_w_out,
              ffn_w_up, ffn_conv_w, ffn_conv_b, ffn_w_down, ln_g, ln_b):
    B, S, _ = x.shape
    HD = N_HEADS * HEAD_DIM
    k_sh = None
    v_sh = None
    for layer in range(DEPTH):
        if layer < N_A:
            mix = rg_lru_block(x, lru_w_in[layer], lru_b_in[layer], lru_conv_w[layer],
                               lru_conv_b[layer], lru_w_gates[layer], lru_b_gates[layer],
                               lru_a_param[layer], lru_w_out[layer], lru_b_out[layer])
        else:
            if layer == N_A:
                kv = x @ kv_w
                k_sh = kv[..., :HD].reshape(B, S, N_HEADS, HEAD_DIM)
                v_sh = kv[..., HD:].reshape(B, S, N_HEADS, HEAD_DIM)
            j = layer - N_A
            q = (x @ attn_w_q[j]).reshape(B, S, N_HEADS, HEAD_DIM)
            o = stick_breaking_attention(q, k_sh, v_sh).reshape(B, S, HD)
            mix = o @ attn_w_out[j]
        x = layer_norm(DN_ALPHA * x + mix, ln_g[layer, 0], ln_b[layer, 0])
        f = conv_ffn(x, ffn_w_up[layer], ffn_conv_w[layer], ffn_conv_b[layer], ffn_w_down[layer])
        x = layer_norm(DN_ALPHA * x + f, ln_g[layer, 1], ln_b[layer, 1])
    return x
```

```python
import functools
import math

import jax
import jax.numpy as jnp
from jax import lax
from jax.experimental import pallas as pl
from jax.experimental.pallas import tpu as pltpu

D_MODEL = 1024
DEPTH = 2
LRU_WIDTH = D_MODEL
LRU_HEADS = 8
LRU_BLOCK = LRU_WIDTH // LRU_HEADS
LRU_CONV = 4
LRU_C = 8.0
N_HEADS = 8
HEAD_DIM = D_MODEL // N_HEADS
D_FF = 3 * D_MODEL
FFN_CONV = 3
DN_ALPHA = (2 * DEPTH) ** 0.25
LN_EPS = 1e-5

SUBLANES = 8
TOKEN_BLOCK = 512
FF_CHUNK = 512
ATTN_Q_BLOCK = 256
ATTN_K_BLOCK = 256
VMEM_LIMIT = 52 * 1024 * 1024

_BF16 = jnp.bfloat16
_F32 = jnp.float32


def _dot(a, b):
    return jnp.dot(a, b, preferred_element_type=_F32)


def _gelu(x):
    c = math.sqrt(2.0 / math.pi)
    return 0.5 * x * (1.0 + jnp.tanh(c * (x + 0.044715 * (x * x * x))))


def _sigmoid(x):
    return 0.5 * (1.0 + jnp.tanh(0.5 * x))


def _layer_norm(y, g, b):
    mu = jnp.mean(y, axis=-1, keepdims=True)
    yc = y - mu
    var = jnp.mean(yc * yc, axis=-1, keepdims=True)
    return yc * lax.rsqrt(var + LN_EPS) * g + b


def _const_spec(shape):
    zeros = (0,) * len(shape)
    return pl.BlockSpec(shape, lambda *_: zeros, pipeline_mode=pl.Buffered(1))


def _params(sem):
    return pltpu.CompilerParams(dimension_semantics=sem, vmem_limit_bytes=VMEM_LIMIT)


def _lru_kernel(x_ref, win_ref, bin_ref, cw_ref, cb_ref, wg_ref, bg_ref, ap_ref,
                wout_ref, bout_ref, g_ref, b_ref, o_ref,
                cbuf, a_buf, u_buf, h_buf, h_carry):
    tm = x_ref.shape[0]
    W = LRU_WIDTH
    s = pl.program_id(1)

    @pl.when(s == 0)
    def _():
        cbuf[0:SUBLANES, :] = jnp.zeros((SUBLANES, W), _F32)
        h_carry[...] = jnp.zeros((SUBLANES, W), _F32)

    x = x_ref[...]
    proj = _dot(x.astype(_BF16), win_ref[...]) + bin_ref[...]
    xr = proj[:, W:]
    cbuf[SUBLANES:SUBLANES + tm, :] = xr
    x_br = (cb_ref[...] + cw_ref[3:4, :] * xr
            + cw_ref[2:3, :] * cbuf[SUBLANES - 1:SUBLANES - 1 + tm, :]
            + cw_ref[1:2, :] * cbuf[SUBLANES - 2:SUBLANES - 2 + tm, :]
            + cw_ref[0:1, :] * cbuf[SUBLANES - 3:SUBLANES - 3 + tm, :])
    cbuf[0:SUBLANES, :] = xr[tm - SUBLANES:tm, :]

    row = lax.broadcasted_iota(jnp.int32, (tm, LRU_BLOCK), 0)
    is_start = jnp.logical_and(row == 0, s == 0)
    neg_c_sp = -LRU_C * jnp.logaddexp(-ap_ref[...], 0.0)
    for n in range(LRU_HEADS):
        cols = slice(n * LRU_BLOCK, (n + 1) * LRU_BLOCK)
        xn = x_br[:, cols]
        gates = _sigmoid(_dot(xn.astype(_BF16), wg_ref[n]) + bg_ref[n])
        gate_i = gates[:, :LRU_BLOCK]
        gate_r = gates[:, LRU_BLOCK:]
        a = jnp.exp(gate_r * neg_c_sp[:, cols])
        mult = jnp.sqrt(jnp.maximum(1.0 - a * a, 0.0))
        mult = jnp.where(is_start, 1.0, mult)
        a_buf[:, cols] = a
        u_buf[:, cols] = mult * gate_i * xn

    sub = lax.broadcasted_iota(jnp.int32, (SUBLANES, W), 0)

    def group(gi, h_prev):
        r = pl.multiple_of(gi * SUBLANES, SUBLANES)
        a = a_buf[pl.ds(r, SUBLANES), :]
        u = u_buf[pl.ds(r, SUBLANES), :]
        for d in (1, 2, 4):
            keep = sub >= d
            u = u + a * jnp.where(keep, pltpu.roll(u, d, 0), 0.0)
            a = a * jnp.where(keep, pltpu.roll(a, d, 0), 1.0)
        h = u + a * h_prev
        h_buf[pl.ds(r, SUBLANES), :] = h
        return jnp.broadcast_to(h[SUBLANES - 1:SUBLANES, :], (SUBLANES, W))

    h_carry[...] = lax.fori_loop(0, tm // SUBLANES, group, h_carry[...])

    y_br = _gelu(proj[:, :W])
    mix = _dot((h_buf[...] * y_br).astype(_BF16), wout_ref[...]) + bout_ref[...]
    o_ref[...] = _layer_norm(DN_ALPHA * x + mix, g_ref[...], b_ref[...])


def _lru_layer(x2d, batch, seq, win, b_in, cw, cb, wg, bg, ap, wout, bout, g, b):
    tm = TOKEN_BLOCK
    W = LRU_WIDTH
    ns = seq // tm
    tok = pl.BlockSpec((tm, D_MODEL), lambda bi, si: (bi * ns + si, 0))
    return pl.pallas_call(
        _lru_kernel,
        grid=(batch, ns),
        in_specs=[tok, _const_spec(win.shape), _const_spec(b_in.shape), _const_spec(cw.shape),
                  _const_spec(cb.shape), _const_spec(wg.shape), _const_spec(bg.shape),
                  _const_spec(ap.shape), _const_spec(wout.shape), _const_spec(bout.shape),
                  _const_spec(g.shape), _const_spec(b.shape)],
        out_specs=tok,
        out_shape=jax.ShapeDtypeStruct(x2d.shape, _F32),
        scratch_shapes=[pltpu.VMEM((tm + SUBLANES, W), _F32),
                        pltpu.VMEM((tm, W), _F32), pltpu.VMEM((tm, W), _F32),
                        pltpu.VMEM((tm, W), _F32), pltpu.VMEM((SUBLANES, W), _F32)],
        compiler_params=_params(("arbitrary", "arbitrary")),
        name="lru_layer",
    )(x2d, win, b_in, cw, cb, wg, bg, ap, wout, bout, g, b)


def _ffn_kernel(x_ref, wup_ref, cw_ref, cb_ref, wdn_ref, g_ref, b_ref, o_ref,
                hbuf, carry):
    tm = x_ref.shape[0]
    fc = FF_CHUNK
    s = pl.program_id(1)

    @pl.when(s == 0)
    def _():
        carry[...] = jnp.zeros(carry.shape, _F32)

    x = x_ref[...]
    xb = x.astype(_BF16)

    def conv(col0):
        cols = slice(col0, col0 + fc)
        h = _dot(xb, wup_ref[:, cols])
        hbuf[0:SUBLANES, :] = carry[:, cols]
        hbuf[SUBLANES:SUBLANES + tm, :] = h
        carry[:, cols] = h[tm - SUBLANES:tm, :]
        return (cb_ref[:, cols] + cw_ref[2:3, cols] * h
                + cw_ref[1:2, cols] * hbuf[SUBLANES - 1:SUBLANES - 1 + tm, :]
                + cw_ref[0:1, cols] * hbuf[SUBLANES - 2:SUBLANES - 2 + tm, :])

    acc = jnp.zeros((tm, D_MODEL), _F32)
    for j in range(D_FF // fc):
        act = _gelu(conv(j * fc))
        act = act * conv(D_FF + j * fc)
        acc = acc + _dot(act.astype(_BF16), wdn_ref[j * fc:(j + 1) * fc, :])
    o_ref[...] = _layer_norm(DN_ALPHA * x + acc, g_ref[...], b_ref[...])


def _ffn_layer(x2d, batch, seq, wup, cw, cb, wdn, g, b):
    tm = TOKEN_BLOCK
    ns = seq // tm
    tok = pl.BlockSpec((tm, D_MODEL), lambda bi, si: (bi * ns + si, 0))
    return pl.pallas_call(
        _ffn_kernel,
        grid=(batch, ns),
        in_specs=[tok, _const_spec(wup.shape), _const_spec(cw.shape), _const_spec(cb.shape),
                  _const_spec(wdn.shape), _const_spec(g.shape), _const_spec(b.shape)],
        out_specs=tok,
        out_shape=jax.ShapeDtypeStruct(x2d.shape, _F32),
        scratch_shapes=[pltpu.VMEM((tm + SUBLANES, FF_CHUNK), _F32),
                        pltpu.VMEM((SUBLANES, 2 * D_FF), _F32)],
        compiler_params=_params(("arbitrary", "arbitrary")),
        name="conv_ffn",
    )(x2d, wup, cw, cb, wdn, g, b)


def _qkv_kernel(x_ref, w_ref, o_ref):
    hd = N_HEADS * HEAD_DIM
    res = _dot(x_ref[...].astype(_BF16), w_ref[...])
    o_ref[:, :2 * hd] = res[:, :2 * hd].astype(_BF16)
    o_ref[:, 2 * hd:] = (res[:, 2 * hd:] * (HEAD_DIM ** -0.5)).astype(_BF16)


def _qkv_proj(x2d, w):
    tm = TOKEN_BLOCK
    n = w.shape[1]
    return pl.pallas_call(
        _qkv_kernel,
        grid=(x2d.shape[0] // tm,),
        in_specs=[pl.BlockSpec((tm, D_MODEL), lambda i: (i, 0)), _const_spec(w.shape)],
        out_specs=pl.BlockSpec((tm, n), lambda i: (i, 0)),
        out_shape=jax.ShapeDtypeStruct((x2d.shape[0], n), _BF16),
        compiler_params=_params(("arbitrary",)),
        name="kvq_proj",
    )(x2d, w)


def _attn_kernel(q_ref, k_ref, v_ref, o_ref):
    tq, tk = ATTN_Q_BLOCK, ATTN_K_BLOCK
    i = pl.program_id(2)
    q = q_ref[...]
    row = lax.broadcasted_iota(jnp.int32, (tq, tk), 0)
    col = lax.broadcasted_iota(jnp.int32, (tq, tk), 1)
    later = (row > col).astype(_BF16)
    causal = col < row

    def logits(j):
        k = k_ref[pl.ds(pl.multiple_of(j * tk, tk), tk), :]
        z = lax.dot_general(q, k, (((1,), (1,)), ((), ())), preferred_element_type=_F32)
        log_beta = jnp.minimum(z, 0.0) - jnp.log(1.0 + jnp.exp(-jnp.abs(z)))
        return log_beta, log_beta - z

    def values(j):
        return v_ref[pl.ds(pl.multiple_of(j * tk, tk), tk), :]

    log_beta, log_1m = logits(i)
    log_1m = jnp.where(causal, log_1m, 0.0)
    suffix = _dot(log_1m.astype(_BF16), later)
    w = jnp.where(causal, jnp.exp(log_beta + suffix), 0.0)
    o = _dot(w.astype(_BF16), values(i))
    tail = jnp.sum(log_1m, axis=1, keepdims=True)

    def body(it, state):
        o, tail = state
        j = i - 1 - it
        log_beta, log_1m = logits(j)
        suffix = _dot(log_1m.astype(_BF16), later) + tail
        w = jnp.exp(log_beta + suffix)
        o = o + _dot(w.astype(_BF16), values(j))
        return o, tail + jnp.sum(log_1m, axis=1, keepdims=True)

    o, _ = lax.fori_loop(0, i, body, (o, tail))
    o_ref[...] = o.astype(o_ref.dtype)


def _attention(kvq, batch, seq):
    tq = ATTN_Q_BLOCK
    nq = seq // tq
    return pl.pallas_call(
        _attn_kernel,
        grid=(batch, N_HEADS, nq),
        in_specs=[pl.BlockSpec((tq, HEAD_DIM), lambda b, h, i: (b * nq + i, 2 * N_HEADS + h)),
                  pl.BlockSpec((seq, HEAD_DIM), lambda b, h, i: (b, h)),
                  pl.BlockSpec((seq, HEAD_DIM), lambda b, h, i: (b, N_HEADS + h))],
        out_specs=pl.BlockSpec((tq, HEAD_DIM), lambda b, h, i: (b * nq + i, h)),
        out_shape=jax.ShapeDtypeStruct((batch * seq, N_HEADS * HEAD_DIM), _BF16),
        compiler_params=_params(("arbitrary", "arbitrary", "arbitrary")),
        name="stickbreak_attn",
    )(kvq, kvq, kvq)


def _proj_ln_kernel(o_ref, x_ref, w_ref, g_ref, b_ref, y_ref):
    mix = _dot(o_ref[...], w_ref[...])
    y_ref[...] = _layer_norm(DN_ALPHA * x_ref[...] + mix, g_ref[...], b_ref[...])


def _proj_ln(o2d, x2d, w, g, b):
    tm = TOKEN_BLOCK
    tok = lambda i: (i, 0)
    return pl.pallas_call(
        _proj_ln_kernel,
        grid=(x2d.shape[0] // tm,),
        in_specs=[pl.BlockSpec((tm, o2d.shape[1]), tok), pl.BlockSpec((tm, D_MODEL), tok),
                  _const_spec(w.shape), _const_spec(g.shape), _const_spec(b.shape)],
        out_specs=pl.BlockSpec((tm, D_MODEL), tok),
        out_shape=jax.ShapeDtypeStruct(x2d.shape, _F32),
        compiler_params=_params(("arbitrary",)),
        name="attn_out_ln",
    )(o2d, x2d, w, g, b)


def kernel(x, lru_w_in, lru_b_in, lru_conv_w, lru_conv_b, lru_w_gates, lru_b_gates, lru_a_param, lru_w_out, lru_b_out, kv_w, attn_w_q, attn_w_out, ffn_w_up, ffn_conv_w, ffn_conv_b, ffn_w_down, ln_g, ln_b):
    batch, seq, d = x.shape
    assert d == D_MODEL and seq % TOKEN_BLOCK == 0 and seq % ATTN_Q_BLOCK == 0
    assert lru_w_in.shape[0] == 1 and attn_w_q.shape[0] == 1 and ffn_w_up.shape[0] == DEPTH
    row = lambda v: v.reshape(1, -1)
    h = x.reshape(batch * seq, d)

    def ffn(h, layer):
        return _ffn_layer(h, batch, seq, ffn_w_up[layer].astype(_BF16), ffn_conv_w[layer],
                          row(ffn_conv_b[layer]), ffn_w_down[layer].astype(_BF16),
                          row(ln_g[layer, 1]), row(ln_b[layer, 1]))

    h = _lru_layer(h, batch, seq, lru_w_in[0].astype(_BF16), row(lru_b_in[0]), lru_conv_w[0],
                   row(lru_conv_b[0]), lru_w_gates[0].astype(_BF16),
                   lru_b_gates[0].reshape(LRU_HEADS, 1, 2 * LRU_BLOCK), row(lru_a_param[0]),
                   lru_w_out[0].astype(_BF16), row(lru_b_out[0]), row(ln_g[0, 0]), row(ln_b[0, 0]))
    h = ffn(h, 0)
    kvq = _qkv_proj(h, jnp.concatenate([kv_w, attn_w_q[0]], axis=1).astype(_BF16))
    o = _attention(kvq, batch, seq)
    h = _proj_ln(o, h, attn_w_out[0].astype(_BF16), row(ln_g[1, 0]), row(ln_b[1, 0]))
    h = ffn(h, 1)
    return h.reshape(batch, seq, d)
```

```python
import functools
import math

import jax
import jax.numpy as jnp
from jax import lax
from jax.experimental import pallas as pl
from jax.experimental.pallas import tpu as pltpu

D_MODEL = 1024
DEPTH = 2
LRU_WIDTH = D_MODEL
LRU_HEADS = 8
LRU_BLOCK = LRU_WIDTH // LRU_HEADS
LRU_CONV = 4
LRU_C = 8.0
N_HEADS = 8
HEAD_DIM = D_MODEL // N_HEADS
D_FF = 3 * D_MODEL
FFN_CONV = 3
DN_ALPHA = (2 * DEPTH) ** 0.25
LN_EPS = 1e-5

SUBLANES = 8
TOKEN_BLOCK = 512
FF_CHUNK = 512
ATTN_Q_BLOCK = 256
ATTN_K_BLOCK = 256
EXP_ZERO_CUTOFF = -104.0
VMEM_LIMIT = 52 * 1024 * 1024

_BF16 = jnp.bfloat16
_F32 = jnp.float32


def _dot(a, b):
    return jnp.dot(a, b, preferred_element_type=_F32)


def _gelu_x2(x):
    c = math.sqrt(2.0 / math.pi)
    t = jnp.tanh(x * (c + (c * 0.044715) * (x * x)))
    return x + x * t


def _sigmoid(x):
    return 0.5 * (1.0 + jnp.tanh(0.5 * x))


def _layer_norm(y, g, b):
    mu = jnp.mean(y, axis=-1, keepdims=True)
    yc = y - mu
    var = jnp.mean(yc * yc, axis=-1, keepdims=True)
    return yc * lax.rsqrt(var + LN_EPS) * g + b


def _const_spec(shape):
    zeros = (0,) * len(shape)
    return pl.BlockSpec(shape, lambda *_: zeros, pipeline_mode=pl.Buffered(1))


def _params(sem):
    return pltpu.CompilerParams(dimension_semantics=sem, vmem_limit_bytes=VMEM_LIMIT)


def _lru_kernel(x_ref, win_ref, bin_ref, cw_ref, cb_ref, wg_ref, bg_ref, ap_ref,
                wout_ref, bout_ref, g_ref, b_ref, o_ref,
                cbuf, a_buf, u_buf, h_buf, h_carry):
    tm = x_ref.shape[0]
    W = LRU_WIDTH
    s = pl.program_id(1)

    @pl.when(s == 0)
    def _():
        cbuf[0:SUBLANES, :] = jnp.zeros((SUBLANES, W), _F32)
        h_carry[...] = jnp.zeros((SUBLANES, W), _F32)

    x = x_ref[...]
    proj = _dot(x.astype(_BF16), win_ref[...]) + bin_ref[...]
    xr = proj[:, W:]
    cbuf[SUBLANES:SUBLANES + tm, :] = xr
    x_br = (cb_ref[...] + cw_ref[3:4, :] * xr
            + cw_ref[2:3, :] * cbuf[SUBLANES - 1:SUBLANES - 1 + tm, :]
            + cw_ref[1:2, :] * cbuf[SUBLANES - 2:SUBLANES - 2 + tm, :]
            + cw_ref[0:1, :] * cbuf[SUBLANES - 3:SUBLANES - 3 + tm, :])
    cbuf[0:SUBLANES, :] = xr[tm - SUBLANES:tm, :]

    row = lax.broadcasted_iota(jnp.int32, (tm, LRU_BLOCK), 0)
    is_start = jnp.logical_and(row == 0, s == 0)
    neg_c_sp = -LRU_C * jnp.logaddexp(-ap_ref[...], 0.0)
    for n in range(LRU_HEADS):
        cols = slice(n * LRU_BLOCK, (n + 1) * LRU_BLOCK)
        xn = x_br[:, cols]
        gates = _sigmoid(_dot(xn.astype(_BF16), wg_ref[n]) + bg_ref[n])
        gate_i = gates[:, :LRU_BLOCK]
        gate_r = gates[:, LRU_BLOCK:]
        a = jnp.exp(gate_r * neg_c_sp[:, cols])
        mult = jnp.sqrt(jnp.maximum(1.0 - a * a, 0.0))
        mult = jnp.where(is_start, 1.0, mult)
        a_buf[:, cols] = a
        u_buf[:, cols] = mult * gate_i * xn

    sub = lax.broadcasted_iota(jnp.int32, (SUBLANES, W), 0)

    def group(gi, h_prev):
        r = pl.multiple_of(gi * SUBLANES, SUBLANES)
        a = a_buf[pl.ds(r, SUBLANES), :]
        u = u_buf[pl.ds(r, SUBLANES), :]
        for d in (1, 2, 4):
            keep = sub >= d
            u = u + a * jnp.where(keep, pltpu.roll(u, d, 0), 0.0)
            a = a * jnp.where(keep, pltpu.roll(a, d, 0), 1.0)
        h = u + a * h_prev
        h_buf[pl.ds(r, SUBLANES), :] = h
        return jnp.broadcast_to(h[SUBLANES - 1:SUBLANES, :], (SUBLANES, W))

    h_carry[...] = lax.fori_loop(0, tm // SUBLANES, group, h_carry[...])

    y_br = _gelu_x2(proj[:, :W])
    mix = _dot((h_buf[...] * y_br).astype(_BF16), wout_ref[...]) + bout_ref[...]
    o_ref[...] = _layer_norm(DN_ALPHA * x + mix, g_ref[...], b_ref[...])


def _lru_layer(x2d, batch, seq, win, b_in, cw, cb, wg, bg, ap, wout, bout, g, b):
    tm = TOKEN_BLOCK
    W = LRU_WIDTH
    ns = seq // tm
    tok = pl.BlockSpec((tm, D_MODEL), lambda bi, si: (bi * ns + si, 0))
    return pl.pallas_call(
        _lru_kernel,
        grid=(batch, ns),
        in_specs=[tok, _const_spec(win.shape), _const_spec(b_in.shape), _const_spec(cw.shape),
                  _const_spec(cb.shape), _const_spec(wg.shape), _const_spec(bg.shape),
                  _const_spec(ap.shape), _const_spec(wout.shape), _const_spec(bout.shape),
                  _const_spec(g.shape), _const_spec(b.shape)],
        out_specs=tok,
        out_shape=jax.ShapeDtypeStruct(x2d.shape, _F32),
        scratch_shapes=[pltpu.VMEM((tm + SUBLANES, W), _F32),
                        pltpu.VMEM((tm, W), _F32), pltpu.VMEM((tm, W), _F32),
                        pltpu.VMEM((tm, W), _F32), pltpu.VMEM((SUBLANES, W), _F32)],
        compiler_params=_params(("arbitrary", "arbitrary")),
        name="lru_layer",
    )(x2d, win, b_in, cw, cb, wg, bg, ap, wout, bout, g, b)


def _shift_rows(h, prev8, d):
    sub = lax.broadcasted_iota(jnp.int32, prev8.shape, 0)
    rolled = pltpu.roll(h, d, 0)
    top = jnp.where(sub < d, pltpu.roll(prev8, d, 0), rolled[0:SUBLANES])
    return jnp.concatenate([top, rolled[SUBLANES:]], axis=0)


def _ffn_kernel(x_ref, wup_ref, cw_ref, cb_ref, wdn_ref, g_ref, b_ref, o_ref, carry):
    tm = x_ref.shape[0]
    fc = FF_CHUNK
    s = pl.program_id(1)

    @pl.when(s == 0)
    def _():
        carry[...] = jnp.zeros(carry.shape, _F32)

    x = x_ref[...]
    xb = x.astype(_BF16)

    def up(j):
        return tuple(_dot(xb, wup_ref[:, c0:c0 + fc]) for c0 in (j * fc, D_FF + j * fc))

    def conv(h, col0):
        cols = slice(col0, col0 + fc)
        prev8 = carry[:, cols]
        carry[:, cols] = h[tm - SUBLANES:tm, :]
        return (cb_ref[:, cols] + cw_ref[2:3, cols] * h
                + cw_ref[1:2, cols] * _shift_rows(h, prev8, 1)
                + cw_ref[0:1, cols] * _shift_rows(h, prev8, 2))

    n_chunks = D_FF // fc
    acc = jnp.zeros((tm, D_MODEL), _F32)
    h_next = up(0)
    for j in range(n_chunks):
        h_gate, h_lin = h_next
        if j + 1 < n_chunks:
            h_next = up(j + 1)
        act = _gelu_x2(conv(h_gate, j * fc)) * conv(h_lin, D_FF + j * fc)
        acc = acc + _dot(act.astype(_BF16), wdn_ref[j * fc:(j + 1) * fc, :])
    o_ref[...] = _layer_norm(DN_ALPHA * x + acc, g_ref[...], b_ref[...])


def _ffn_layer(x2d, batch, seq, wup, cw, cb, wdn, g, b):
    tm = TOKEN_BLOCK
    ns = seq // tm
    tok = pl.BlockSpec((tm, D_MODEL), lambda bi, si: (bi * ns + si, 0))
    return pl.pallas_call(
        _ffn_kernel,
        grid=(batch, ns),
        in_specs=[tok, _const_spec(wup.shape), _const_spec(cw.shape), _const_spec(cb.shape),
                  _const_spec(wdn.shape), _const_spec(g.shape), _const_spec(b.shape)],
        out_specs=tok,
        out_shape=jax.ShapeDtypeStruct(x2d.shape, _F32),
        scratch_shapes=[pltpu.VMEM((SUBLANES, 2 * D_FF), _F32)],
        compiler_params=_params(("arbitrary", "arbitrary")),
        name="conv_ffn",
    )(x2d, wup, cw, cb, wdn, g, b)


def _qkv_kernel(x_ref, w_ref, o_ref):
    hd = N_HEADS * HEAD_DIM
    res = _dot(x_ref[...].astype(_BF16), w_ref[...])
    o_ref[:, :2 * hd] = res[:, :2 * hd].astype(_BF16)
    o_ref[:, 2 * hd:] = (res[:, 2 * hd:] * (HEAD_DIM ** -0.5)).astype(_BF16)


def _qkv_proj(x2d, w):
    tm = TOKEN_BLOCK
    n = w.shape[1]
    return pl.pallas_call(
        _qkv_kernel,
        grid=(x2d.shape[0] // tm,),
        in_specs=[pl.BlockSpec((tm, D_MODEL), lambda i: (i, 0)), _const_spec(w.shape)],
        out_specs=pl.BlockSpec((tm, n), lambda i: (i, 0)),
        out_shape=jax.ShapeDtypeStruct((x2d.shape[0], n), _BF16),
        compiler_params=_params(("arbitrary",)),
        name="kvq_proj",
    )(x2d, w)


def _attn_kernel(q_ref, k_ref, v_ref, o_ref):
    tq, tk = ATTN_Q_BLOCK, ATTN_K_BLOCK
    i = pl.program_id(2)
    q = q_ref[...]
    row = lax.broadcasted_iota(jnp.int32, (tq, tk), 0)
    col = lax.broadcasted_iota(jnp.int32, (tq, tk), 1)
    later = (row > col).astype(_BF16)
    causal = col < row

    def logits(j):
        k = k_ref[pl.ds(pl.multiple_of(j * tk, tk), tk), :]
        z = lax.dot_general(q, k, (((1,), (1,)), ((), ())), preferred_element_type=_F32)
        log_beta = jnp.minimum(z, 0.0) - jnp.log(1.0 + jnp.exp(-jnp.abs(z)))
        return log_beta, log_beta - z

    def values(j):
        return v_ref[pl.ds(pl.multiple_of(j * tk, tk), tk), :]

    def diagonal():
        log_beta, log_1m = logits(i)
        log_1m = jnp.where(causal, log_1m, 0.0)
        suffix = _dot(log_1m.astype(_BF16), later)
        w = jnp.where(causal, jnp.exp(log_beta + suffix), 0.0)
        return _dot(w.astype(_BF16), values(i)), jnp.sum(log_1m, axis=1, keepdims=True)

    def earlier(j, o, tail):
        log_beta, log_1m = logits(j)
        suffix = _dot(log_1m.astype(_BF16), later) + tail
        w = jnp.exp(log_beta + suffix)
        return o + _dot(w.astype(_BF16), values(j)), tail + jnp.sum(log_1m, axis=1, keepdims=True)

    def alive(tail):
        return (jnp.max(tail) >= EXP_ZERO_CUTOFF).astype(jnp.int32)

    @pl.when(i == 0)
    def _():
        o, _ = diagonal()
        o_ref[...] = o.astype(o_ref.dtype)

    @pl.when(i > 0)
    def _():
        o, tail = diagonal()
        o, tail = earlier(i - 1, o, tail)

        def cond(state):
            j, _, _, live = state
            return jnp.logical_and(j >= 0, live > 0)

        def body(state):
            j, o, tail, _ = state
            o, tail = earlier(j, o, tail)
            return j - 1, o, tail, alive(tail)

        _, o, _, _ = lax.while_loop(cond, body, (i - 2, o, tail, alive(tail)))
        o_ref[...] = o.astype(o_ref.dtype)


def _attention(kvq, batch, seq):
    tq = ATTN_Q_BLOCK
    nq = seq // tq
    return pl.pallas_call(
        _attn_kernel,
        grid=(batch, N_HEADS, nq),
        in_specs=[pl.BlockSpec((tq, HEAD_DIM), lambda b, h, i: (b * nq + i, 2 * N_HEADS + h)),
                  pl.BlockSpec((seq, HEAD_DIM), lambda b, h, i: (b, h)),
                  pl.BlockSpec((seq, HEAD_DIM), lambda b, h, i: (b, N_HEADS + h))],
        out_specs=pl.BlockSpec((tq, HEAD_DIM), lambda b, h, i: (b * nq + i, h)),
        out_shape=jax.ShapeDtypeStruct((batch * seq, N_HEADS * HEAD_DIM), _BF16),
        compiler_params=_params(("arbitrary", "arbitrary", "arbitrary")),
        name="stickbreak_attn",
    )(kvq, kvq, kvq)


def _proj_ln_kernel(o_ref, x_ref, w_ref, g_ref, b_ref, y_ref):
    mix = _dot(o_ref[...], w_ref[...])
    y_ref[...] = _layer_norm(DN_ALPHA * x_ref[...] + mix, g_ref[...], b_ref[...])


def _proj_ln(o2d, x2d, w, g, b):
    tm = TOKEN_BLOCK
    tok = lambda i: (i, 0)
    return pl.pallas_call(
        _proj_ln_kernel,
        grid=(x2d.shape[0] // tm,),
        in_specs=[pl.BlockSpec((tm, o2d.shape[1]), tok), pl.BlockSpec((tm, D_MODEL), tok),
                  _const_spec(w.shape), _const_spec(g.shape), _const_spec(b.shape)],
        out_specs=pl.BlockSpec((tm, D_MODEL), tok),
        out_shape=jax.ShapeDtypeStruct(x2d.shape, _F32),
        compiler_params=_params(("arbitrary",)),
        name="attn_out_ln",
    )(o2d, x2d, w, g, b)


def kernel(x, lru_w_in, lru_b_in, lru_conv_w, lru_conv_b, lru_w_gates, lru_b_gates, lru_a_param, lru_w_out, lru_b_out, kv_w, attn_w_q, attn_w_out, ffn_w_up, ffn_conv_w, ffn_conv_b, ffn_w_down, ln_g, ln_b):
    batch, seq, d = x.shape
    assert d == D_MODEL and seq % TOKEN_BLOCK == 0 and seq % ATTN_Q_BLOCK == 0
    assert lru_w_in.shape[0] == 1 and attn_w_q.shape[0] == 1 and ffn_w_up.shape[0] == DEPTH
    row = lambda v: v.reshape(1, -1)
    h = x.reshape(batch * seq, d)

    half_lin = jnp.concatenate([jnp.ones((D_FF,), _F32), jnp.full((D_FF,), 0.5, _F32)])

    def ffn(h, layer):
        return _ffn_layer(h, batch, seq, ffn_w_up[layer].astype(_BF16), ffn_conv_w[layer] * half_lin,
                          row(ffn_conv_b[layer] * half_lin), ffn_w_down[layer].astype(_BF16),
                          row(ln_g[layer, 1]), row(ln_b[layer, 1]))

    h = _lru_layer(h, batch, seq, lru_w_in[0].astype(_BF16), row(lru_b_in[0]), lru_conv_w[0],
                   row(lru_conv_b[0]), lru_w_gates[0].astype(_BF16),
                   lru_b_gates[0].reshape(LRU_HEADS, 1, 2 * LRU_BLOCK), row(lru_a_param[0]),
                   (0.5 * lru_w_out[0]).astype(_BF16), row(lru_b_out[0]), row(ln_g[0, 0]), row(ln_b[0, 0]))
    h = ffn(h, 0)
    kvq = _qkv_proj(h, jnp.concatenate([kv_w, attn_w_q[0]], axis=1).astype(_BF16))
    o = _attention(kvq, batch, seq)
    h = _proj_ln(o, h, attn_w_out[0].astype(_BF16), row(ln_g[1, 0]), row(ln_b[1, 0]))
    h = ffn(h, 1)
    return h.reshape(batch, seq, d)
```

```python
import functools
import math

import jax
import jax.numpy as jnp
from jax import lax
from jax.experimental import pallas as pl
from jax.experimental.pallas import tpu as pltpu

D_MODEL = 1024
DEPTH = 2
LRU_WIDTH = D_MODEL
LRU_HEADS = 8
LRU_BLOCK = LRU_WIDTH // LRU_HEADS
LRU_CONV = 4
LRU_C = 8.0
N_HEADS = 8
HEAD_DIM = D_MODEL // N_HEADS
D_FF = 3 * D_MODEL
FFN_CONV = 3
DN_ALPHA = (2 * DEPTH) ** 0.25
LN_EPS = 1e-5

SUBLANES = 8
TOKEN_BLOCK = 512
FF_CHUNK = 512
ATTN_Q_BLOCK = 256
ATTN_K_BLOCK = 256
EXP_ZERO_CUTOFF = -104.0
VMEM_LIMIT = 52 * 1024 * 1024

_BF16 = jnp.bfloat16
_F32 = jnp.float32


def _dot(a, b):
    return jnp.dot(a, b, preferred_element_type=_F32)


def _gelu_x2(x):
    c = math.sqrt(2.0 / math.pi)
    t = jnp.tanh(x * (c + (c * 0.044715) * (x * x)))
    return x + x * t


def _sigmoid(x):
    return 0.5 * (1.0 + jnp.tanh(0.5 * x))


def _layer_norm(y, g, b):
    mu = jnp.mean(y, axis=-1, keepdims=True)
    yc = y - mu
    var = jnp.mean(yc * yc, axis=-1, keepdims=True)
    return yc * lax.rsqrt(var + LN_EPS) * g + b


def _const_spec(shape):
    zeros = (0,) * len(shape)
    return pl.BlockSpec(shape, lambda *_: zeros, pipeline_mode=pl.Buffered(1))


def _params(sem):
    return pltpu.CompilerParams(dimension_semantics=sem, vmem_limit_bytes=VMEM_LIMIT)


def _lru_kernel(x_ref, win_ref, bin_ref, cw_ref, cb_ref, wg_ref, bg_ref, ap_ref,
                wout_ref, bout_ref, g_ref, b_ref, o_ref,
                cbuf, a_buf, u_buf, h_buf, h_carry):
    tm = x_ref.shape[0]
    W = LRU_WIDTH
    s = pl.program_id(1)

    @pl.when(s == 0)
    def _():
        cbuf[0:SUBLANES, :] = jnp.zeros((SUBLANES, W), _F32)
        h_carry[...] = jnp.zeros((SUBLANES, W), _F32)

    x = x_ref[...]
    proj = _dot(x.astype(_BF16), win_ref[...]) + bin_ref[...]
    xr = proj[:, W:]
    cbuf[SUBLANES:SUBLANES + tm, :] = xr
    x_br = (cb_ref[...] + cw_ref[3:4, :] * xr
            + cw_ref[2:3, :] * cbuf[SUBLANES - 1:SUBLANES - 1 + tm, :]
            + cw_ref[1:2, :] * cbuf[SUBLANES - 2:SUBLANES - 2 + tm, :]
            + cw_ref[0:1, :] * cbuf[SUBLANES - 3:SUBLANES - 3 + tm, :])
    cbuf[0:SUBLANES, :] = xr[tm - SUBLANES:tm, :]

    row = lax.broadcasted_iota(jnp.int32, (tm, LRU_BLOCK), 0)
    is_start = jnp.logical_and(row == 0, s == 0)
    neg_c_sp = -LRU_C * jnp.logaddexp(-ap_ref[...], 0.0)
    for n in range(LRU_HEADS):
        cols = slice(n * LRU_BLOCK, (n + 1) * LRU_BLOCK)
        xn = x_br[:, cols]
        gates = _sigmoid(_dot(xn.astype(_BF16), wg_ref[n]) + bg_ref[n])
        gate_i = gates[:, :LRU_BLOCK]
        gate_r = gates[:, LRU_BLOCK:]
        a = jnp.exp(gate_r * neg_c_sp[:, cols])
        mult = jnp.sqrt(jnp.maximum(1.0 - a * a, 0.0))
        mult = jnp.where(is_start, 1.0, mult)
        a_buf[:, cols] = a
        u_buf[:, cols] = mult * gate_i * xn

    sub = lax.broadcasted_iota(jnp.int32, (SUBLANES, W), 0)

    def group(gi, h_prev):
        r = pl.multiple_of(gi * SUBLANES, SUBLANES)
        a = a_buf[pl.ds(r, SUBLANES), :]
        u = u_buf[pl.ds(r, SUBLANES), :]
        for d in (1, 2, 4):
            keep = sub >= d
            u = u + a * jnp.where(keep, pltpu.roll(u, d, 0), 0.0)
            a = a * jnp.where(keep, pltpu.roll(a, d, 0), 1.0)
        h = u + a * h_prev
        h_buf[pl.ds(r, SUBLANES), :] = h
        return jnp.broadcast_to(h[SUBLANES - 1:SUBLANES, :], (SUBLANES, W))

    h_carry[...] = lax.fori_loop(0, tm // SUBLANES, group, h_carry[...])

    y_br = _gelu_x2(proj[:, :W])
    mix = _dot((h_buf[...] * y_br).astype(_BF16), wout_ref[...]) + bout_ref[...]
    o_ref[...] = _layer_norm(DN_ALPHA * x + mix, g_ref[...], b_ref[...])


def _lru_layer(x2d, batch, seq, win, b_in, cw, cb, wg, bg, ap, wout, bout, g, b):
    tm = TOKEN_BLOCK
    W = LRU_WIDTH
    ns = seq // tm
    tok = pl.BlockSpec((tm, D_MODEL), lambda bi, si: (bi * ns + si, 0))
    return pl.pallas_call(
        _lru_kernel,
        grid=(batch, ns),
        in_specs=[tok, _const_spec(win.shape), _const_spec(b_in.shape), _const_spec(cw.shape),
                  _const_spec(cb.shape), _const_spec(wg.shape), _const_spec(bg.shape),
                  _const_spec(ap.shape), _const_spec(wout.shape), _const_spec(bout.shape),
                  _const_spec(g.shape), _const_spec(b.shape)],
        out_specs=tok,
        out_shape=jax.ShapeDtypeStruct(x2d.shape, _F32),
        scratch_shapes=[pltpu.VMEM((tm + SUBLANES, W), _F32),
                        pltpu.VMEM((tm, W), _F32), pltpu.VMEM((tm, W), _F32),
                        pltpu.VMEM((tm, W), _F32), pltpu.VMEM((SUBLANES, W), _F32)],
        compiler_params=_params(("arbitrary", "arbitrary")),
        name="lru_layer",
    )(x2d, win, b_in, cw, cb, wg, bg, ap, wout, bout, g, b)


def _shift_rows(h, prev8, d):
    sub = lax.broadcasted_iota(jnp.int32, prev8.shape, 0)
    rolled = pltpu.roll(h, d, 0)
    top = jnp.where(sub < d, pltpu.roll(prev8, d, 0), rolled[0:SUBLANES])
    return jnp.concatenate([top, rolled[SUBLANES:]], axis=0)


def _ffn_kernel(x_ref, wup_ref, cw_ref, cb_ref, wdn_ref, g_ref, b_ref, o_ref, carry):
    tm = x_ref.shape[0]
    fc = FF_CHUNK
    s = pl.program_id(1)

    @pl.when(s == 0)
    def _():
        carry[...] = jnp.zeros(carry.shape, _F32)

    x = x_ref[...]
    xb = x.astype(_BF16)

    def up(c0):
        return _dot(xb, wup_ref[:, c0:c0 + fc])

    def conv(h, col0):
        cols = slice(col0, col0 + fc)
        prev8 = carry[:, cols]
        carry[:, cols] = h[tm - SUBLANES:tm, :]
        return (cb_ref[:, cols] + cw_ref[2:3, cols] * h
                + cw_ref[1:2, cols] * _shift_rows(h, prev8, 1)
                + cw_ref[0:1, cols] * _shift_rows(h, prev8, 2))

    n_chunks = D_FF // fc
    acc = jnp.zeros((tm, D_MODEL), _F32)
    h_gate, h_lin = up(0), up(D_FF)
    for j in range(n_chunks):
        if j + 1 < n_chunks:
            next_gate, next_lin = up((j + 1) * fc), up(D_FF + (j + 1) * fc)
        act = _gelu_x2(conv(h_gate, j * fc)) * conv(h_lin, D_FF + j * fc)
        acc = acc + _dot(act.astype(_BF16), wdn_ref[j * fc:(j + 1) * fc, :])
        h_gate, h_lin = next_gate, next_lin
    o_ref[...] = _layer_norm(DN_ALPHA * x + acc, g_ref[...], b_ref[...])


def _ffn_layer(x2d, batch, seq, wup, cw, cb, wdn, g, b):
    tm = TOKEN_BLOCK
    ns = seq // tm
    tok = pl.BlockSpec((tm, D_MODEL), lambda bi, si: (bi * ns + si, 0))
    return pl.pallas_call(
        _ffn_kernel,
        grid=(batch, ns),
        in_specs=[tok, _const_spec(wup.shape), _const_spec(cw.shape), _const_spec(cb.shape),
                  _const_spec(wdn.shape), _const_spec(g.shape), _const_spec(b.shape)],
        out_specs=tok,
        out_shape=jax.ShapeDtypeStruct(x2d.shape, _F32),
        scratch_shapes=[pltpu.VMEM((SUBLANES, 2 * D_FF), _F32)],
        compiler_params=_params(("arbitrary", "arbitrary")),
        name="conv_ffn",
    )(x2d, wup, cw, cb, wdn, g, b)


def _qkv_kernel(x_ref, w_ref, o_ref):
    res = _dot(x_ref[...].astype(_BF16), w_ref[...])
    for n in range(3 * N_HEADS):
        part = res[:, n * HEAD_DIM:(n + 1) * HEAD_DIM]
        if n >= 2 * N_HEADS:
            part = part * (HEAD_DIM ** -0.5)
        o_ref[n] = part.astype(_BF16)


def _qkv_proj(x2d, w):
    tm = TOKEN_BLOCK
    tokens = x2d.shape[0]
    return pl.pallas_call(
        _qkv_kernel,
        grid=(tokens // tm,),
        in_specs=[pl.BlockSpec((tm, D_MODEL), lambda i: (i, 0)), _const_spec(w.shape)],
        out_specs=pl.BlockSpec((3 * N_HEADS, tm, HEAD_DIM), lambda i: (0, i, 0)),
        out_shape=jax.ShapeDtypeStruct((3 * N_HEADS, tokens, HEAD_DIM), _BF16),
        compiler_params=_params(("arbitrary",)),
        name="kvq_proj",
    )(x2d, w)


def _attn_kernel(q_ref, k_ref, v_ref, o_ref, o_acc, tail_acc):
    tq, tk = ATTN_Q_BLOCK, ATTN_K_BLOCK
    i = pl.program_id(1)
    row = lax.broadcasted_iota(jnp.int32, (tq, tk), 0)
    col = lax.broadcasted_iota(jnp.int32, (tq, tk), 1)
    later = (row > col).astype(_BF16)
    causal = col < row

    def logits(h, j):
        k = k_ref[h, pl.ds(pl.multiple_of(j * tk, tk), tk), :]
        z = lax.dot_general(q_ref[h], k, (((1,), (1,)), ((), ())), preferred_element_type=_F32)
        log_beta = jnp.minimum(z, 0.0) - jnp.log(1.0 + jnp.exp(-jnp.abs(z)))
        return log_beta, log_beta - z

    def values(h, j):
        return v_ref[h, pl.ds(pl.multiple_of(j * tk, tk), tk), :]

    def diagonal(h):
        log_beta, log_1m = logits(h, i)
        log_1m = jnp.where(causal, log_1m, 0.0)
        suffix = _dot(log_1m.astype(_BF16), later)
        w = jnp.where(causal, jnp.exp(log_beta + suffix), 0.0)
        return _dot(w.astype(_BF16), values(h, i)), jnp.sum(log_1m, axis=1, keepdims=True)

    def earlier(h, j, o, tail):
        log_beta, log_1m = logits(h, j)
        suffix = _dot(log_1m.astype(_BF16), later) + tail
        w = jnp.exp(log_beta + suffix)
        return (o + _dot(w.astype(_BF16), values(h, j)),
                tail + jnp.sum(log_1m, axis=1, keepdims=True))

    def alive(tail):
        return jnp.max(tail) >= EXP_ZERO_CUTOFF

    def emit(h, o):
        o_ref[:, h * HEAD_DIM:(h + 1) * HEAD_DIM] = o.astype(o_ref.dtype)

    @pl.when(i == 0)
    def _():
        for h in range(N_HEADS):
            emit(h, diagonal(h)[0])

    @pl.when(i > 0)
    def _():
        worst = jnp.full((tq, 1), -jnp.inf, _F32)
        for h in range(N_HEADS):
            o, tail = diagonal(h)
            o, tail = earlier(h, i - 1, o, tail)
            o_acc[h] = o
            tail_acc[h] = tail
            worst = jnp.maximum(worst, tail)

        @pl.when(jnp.logical_and(i > 1, alive(worst)))
        def _():
            def head(h, _):
                def cond(state):
                    j, _, _, live = state
                    return jnp.logical_and(j >= 0, live > 0)

                def body(state):
                    j, o, tail, _ = state
                    o, tail = earlier(h, j, o, tail)
                    return j - 1, o, tail, alive(tail).astype(jnp.int32)

                tail = tail_acc[h]
                state = (i - 2, o_acc[h], tail, alive(tail).astype(jnp.int32))
                o_acc[h] = lax.while_loop(cond, body, state)[1]
                return 0

            lax.fori_loop(0, N_HEADS, head, 0)

        for h in range(N_HEADS):
            emit(h, o_acc[h])


def _attention(kvq, batch, seq):
    tq = ATTN_Q_BLOCK
    nq = seq // tq
    return pl.pallas_call(
        _attn_kernel,
        grid=(batch, nq),
        in_specs=[pl.BlockSpec((N_HEADS, tq, HEAD_DIM), lambda b, i: (2, b * nq + i, 0)),
                  pl.BlockSpec((N_HEADS, seq, HEAD_DIM), lambda b, i: (0, b, 0)),
                  pl.BlockSpec((N_HEADS, seq, HEAD_DIM), lambda b, i: (1, b, 0))],
        out_specs=pl.BlockSpec((tq, N_HEADS * HEAD_DIM), lambda b, i: (b * nq + i, 0)),
        out_shape=jax.ShapeDtypeStruct((batch * seq, N_HEADS * HEAD_DIM), _BF16),
        scratch_shapes=[pltpu.VMEM((N_HEADS, tq, HEAD_DIM), _F32),
                        pltpu.VMEM((N_HEADS, tq, 1), _F32)],
        compiler_params=_params(("arbitrary", "arbitrary")),
        name="stickbreak_attn",
    )(kvq, kvq, kvq)


def _proj_ln_kernel(o_ref, x_ref, w_ref, g_ref, b_ref, y_ref):
    mix = _dot(o_ref[...], w_ref[...])
    y_ref[...] = _layer_norm(DN_ALPHA * x_ref[...] + mix, g_ref[...], b_ref[...])


def _proj_ln(o2d, x2d, w, g, b):
    tm = TOKEN_BLOCK
    tok = lambda i: (i, 0)
    return pl.pallas_call(
        _proj_ln_kernel,
        grid=(x2d.shape[0] // tm,),
        in_specs=[pl.BlockSpec((tm, o2d.shape[1]), tok), pl.BlockSpec((tm, D_MODEL), tok),
                  _const_spec(w.shape), _const_spec(g.shape), _const_spec(b.shape)],
        out_specs=pl.BlockSpec((tm, D_MODEL), tok),
        out_shape=jax.ShapeDtypeStruct(x2d.shape, _F32),
        compiler_params=_params(("arbitrary",)),
        name="attn_out_ln",
    )(o2d, x2d, w, g, b)


def kernel(x, lru_w_in, lru_b_in, lru_conv_w, lru_conv_b, lru_w_gates, lru_b_gates, lru_a_param, lru_w_out, lru_b_out, kv_w, attn_w_q, attn_w_out, ffn_w_up, ffn_conv_w, ffn_conv_b, ffn_w_down, ln_g, ln_b):
    batch, seq, d = x.shape
    assert d == D_MODEL and seq % TOKEN_BLOCK == 0 and seq % ATTN_Q_BLOCK == 0
    assert lru_w_in.shape[0] == 1 and attn_w_q.shape[0] == 1 and ffn_w_up.shape[0] == DEPTH
    row = lambda v: v.reshape(1, -1)
    h = x.reshape(batch * seq, d)

    half_lin = jnp.concatenate([jnp.ones((D_FF,), _F32), jnp.full((D_FF,), 0.5, _F32)])

    def ffn(h, layer):
        return _ffn_layer(h, batch, seq, ffn_w_up[layer].astype(_BF16), ffn_conv_w[layer] * half_lin,
                          row(ffn_conv_b[layer] * half_lin), ffn_w_down[layer].astype(_BF16),
                          row(ln_g[layer, 1]), row(ln_b[layer, 1]))

    h = _lru_layer(h, batch, seq, lru_w_in[0].astype(_BF16), row(lru_b_in[0]), lru_conv_w[0],
                   row(lru_conv_b[0]), lru_w_gates[0].astype(_BF16),
                   lru_b_gates[0].reshape(LRU_HEADS, 1, 2 * LRU_BLOCK), row(lru_a_param[0]),
                   (0.5 * lru_w_out[0]).astype(_BF16), row(lru_b_out[0]), row(ln_g[0, 0]), row(ln_b[0, 0]))
    h = ffn(h, 0)
    kvq = _qkv_proj(h, jnp.concatenate([kv_w, attn_w_q[0]], axis=1).astype(_BF16))
    o = _attention(kvq, batch, seq)
    h = _proj_ln(o, h, attn_w_out[0].astype(_BF16), row(ln_g[1, 0]), row(ln_b[1, 0]))
    h = ffn(h, 1)
    return h.reshape(batch, seq, d)
```

```python
import functools
import math

import jax
import jax.numpy as jnp
from jax import lax
from jax.experimental import pallas as pl
from jax.experimental.pallas import tpu as pltpu

D_MODEL = 1024
DEPTH = 2
LRU_WIDTH = D_MODEL
LRU_HEADS = 8
LRU_BLOCK = LRU_WIDTH // LRU_HEADS
LRU_CONV = 4
LRU_C = 8.0
N_HEADS = 8
HEAD_DIM = D_MODEL // N_HEADS
D_FF = 3 * D_MODEL
FFN_CONV = 3
DN_ALPHA = (2 * DEPTH) ** 0.25
LN_EPS = 1e-5

SUBLANES = 8
TOKEN_BLOCK = 512
FF_CHUNK = 512
ATTN_Q_BLOCK = 256
ATTN_K_BLOCK = 256
EXP_ZERO_CUTOFF = -104.0
VMEM_LIMIT = 52 * 1024 * 1024

_BF16 = jnp.bfloat16
_F32 = jnp.float32


def _dot(a, b):
    return jnp.dot(a, b, preferred_element_type=_F32)


def _gelu_x2(x):
    c = math.sqrt(2.0 / math.pi)
    t = jnp.tanh(x * (c + (c * 0.044715) * (x * x)))
    return x + x * t


def _layer_norm(y, g, b):
    mu = jnp.mean(y, axis=-1, keepdims=True)
    yc = y - mu
    var = jnp.mean(yc * yc, axis=-1, keepdims=True)
    return yc * lax.rsqrt(var + LN_EPS) * g + b


def _const_spec(shape):
    zeros = (0,) * len(shape)
    return pl.BlockSpec(shape, lambda *_: zeros, pipeline_mode=pl.Buffered(1))


def _params(sem):
    return pltpu.CompilerParams(dimension_semantics=sem, vmem_limit_bytes=VMEM_LIMIT)


def _lru_kernel(x_ref, win_ref, bin_ref, cw_ref, cb_ref, wg_ref, bg_ref, ap_ref,
                wout_ref, bout_ref, g_ref, b_ref, o_ref,
                cbuf, a_buf, u_buf, h_buf, h_carry):
    tm = x_ref.shape[0]
    W = LRU_WIDTH
    s = pl.program_id(1)

    @pl.when(s == 0)
    def _():
        cbuf[0:SUBLANES, :] = jnp.zeros((SUBLANES, W), _F32)
        h_carry[...] = jnp.zeros((SUBLANES, W), _F32)

    hm = tm // 2
    nt = 2 * LRU_BLOCK
    half_c_sp = (-0.5 * LRU_C) * jnp.logaddexp(-ap_ref[...], 0.0)
    sub = lax.broadcasted_iota(jnp.int32, (SUBLANES, W), 0)
    row = lax.broadcasted_iota(jnp.int32, (hm, LRU_BLOCK), 0)

    def in_proj_tile(k, t):
        xk = x_ref[k * hm:(k + 1) * hm, :].astype(_BF16)
        return _dot(xk, win_ref[:, t * nt:(t + 1) * nt]) + bin_ref[:, t * nt:(t + 1) * nt]

    def recurrence(k, proj, h_prev, side_jobs):
        base = SUBLANES + k * hm
        results = []
        for n in range(LRU_HEADS):
            results.append(side_jobs[n]())
            cols = slice(n * LRU_BLOCK, (n + 1) * LRU_BLOCK)
            tile = proj[W // nt + n // 2]
            xr = tile[:, (n % 2) * LRU_BLOCK:(n % 2 + 1) * LRU_BLOCK]
            cbuf[base:base + hm, cols] = xr
            xn = (cb_ref[:, cols] + cw_ref[3:4, cols] * xr
                  + cw_ref[2:3, cols] * cbuf[base - 1:base - 1 + hm, cols]
                  + cw_ref[1:2, cols] * cbuf[base - 2:base - 2 + hm, cols]
                  + cw_ref[0:1, cols] * cbuf[base - 3:base - 3 + hm, cols])
            t = jnp.tanh(_dot(xn.astype(_BF16), wg_ref[n]) + bg_ref[n])
            gate_i = 0.5 + 0.5 * t[:, :LRU_BLOCK]
            a = jnp.exp(half_c_sp[:, cols] + half_c_sp[:, cols] * t[:, LRU_BLOCK:])
            mult = jnp.sqrt(jnp.maximum(1.0 - a * a, 0.0))
            if k == 0:
                mult = jnp.where(jnp.logical_and(row == 0, s == 0), 1.0, mult)
            a_buf[k * hm:(k + 1) * hm, cols] = a
            u_buf[k * hm:(k + 1) * hm, cols] = mult * gate_i * xn

        def group(gi, h_prev):
            r = pl.multiple_of(k * hm + gi * SUBLANES, SUBLANES)
            a = a_buf[pl.ds(r, SUBLANES), :]
            u = u_buf[pl.ds(r, SUBLANES), :]
            for d in (1, 2, 4):
                keep = sub >= d
                u = u + a * jnp.where(keep, pltpu.roll(u, d, 0), 0.0)
                a = a * jnp.where(keep, pltpu.roll(a, d, 0), 1.0)
            h = u + a * h_prev
            h_buf[pl.ds(r, SUBLANES), :] = h
            return jnp.broadcast_to(h[SUBLANES - 1:SUBLANES, :], (SUBLANES, W))

        return lax.fori_loop(0, hm // SUBLANES, group, h_prev), results

    def gated(k, proj):
        y_br = _gelu_x2(jnp.concatenate(proj[:W // nt], axis=1))
        return (h_buf[k * hm:(k + 1) * hm, :] * y_br).astype(_BF16)

    def out_proj_tile(hy, t):
        return _dot(hy, wout_ref[:, t * nt:(t + 1) * nt]) + bout_ref[:, t * nt:(t + 1) * nt]

    def finish(k, mix_tiles):
        rows = slice(k * hm, (k + 1) * hm)
        mix = jnp.concatenate(mix_tiles, axis=1)
        o_ref[rows, :] = _layer_norm(DN_ALPHA * x_ref[rows, :] + mix, g_ref[...], b_ref[...])

    n_tiles = 2 * W // nt
    idle = lambda: None
    proj0 = [in_proj_tile(0, t) for t in range(n_tiles)]
    h_mid, proj1 = recurrence(0, proj0, h_carry[...],
                              [functools.partial(in_proj_tile, 1, t) for t in range(n_tiles)])
    hy0 = gated(0, proj0)
    out_jobs = [functools.partial(out_proj_tile, hy0, t) for t in range(D_MODEL // nt)]
    h_last, mix0 = recurrence(1, proj1, h_mid, out_jobs + [idle] * (LRU_HEADS - len(out_jobs)))
    h_carry[...] = h_last
    finish(0, mix0[:len(out_jobs)])
    hy1 = gated(1, proj1)
    finish(1, [out_proj_tile(hy1, t) for t in range(D_MODEL // nt)])
    cbuf[0:SUBLANES, :] = cbuf[tm:tm + SUBLANES, :]


def _lru_layer(x2d, batch, seq, win, b_in, cw, cb, wg, bg, ap, wout, bout, g, b):
    tm = TOKEN_BLOCK
    W = LRU_WIDTH
    ns = seq // tm
    tok = pl.BlockSpec((tm, D_MODEL), lambda bi, si: (bi * ns + si, 0))
    return pl.pallas_call(
        _lru_kernel,
        grid=(batch, ns),
        in_specs=[tok, _const_spec(win.shape), _const_spec(b_in.shape), _const_spec(cw.shape),
                  _const_spec(cb.shape), _const_spec(wg.shape), _const_spec(bg.shape),
                  _const_spec(ap.shape), _const_spec(wout.shape), _const_spec(bout.shape),
                  _const_spec(g.shape), _const_spec(b.shape)],
        out_specs=tok,
        out_shape=jax.ShapeDtypeStruct(x2d.shape, _F32),
        scratch_shapes=[pltpu.VMEM((tm + SUBLANES, W), _F32),
                        pltpu.VMEM((tm, W), _F32), pltpu.VMEM((tm, W), _F32),
                        pltpu.VMEM((tm, W), _F32), pltpu.VMEM((SUBLANES, W), _F32)],
        compiler_params=_params(("arbitrary", "arbitrary")),
        name="lru_layer",
    )(x2d, win, b_in, cw, cb, wg, bg, ap, wout, bout, g, b)


def _shift_rows(h, prev8, d):
    sub = lax.broadcasted_iota(jnp.int32, prev8.shape, 0)
    rolled = pltpu.roll(h, d, 0)
    top = jnp.where(sub < d, pltpu.roll(prev8, d, 0), rolled[0:SUBLANES])
    return jnp.concatenate([top, rolled[SUBLANES:]], axis=0)


def _ffn_kernel(x_ref, wup_ref, cw_ref, cb_ref, wdn_ref, g_ref, b_ref, o_ref, carry):
    tm = x_ref.shape[0]
    fc = FF_CHUNK
    s = pl.program_id(1)

    @pl.when(s == 0)
    def _():
        carry[...] = jnp.zeros(carry.shape, _F32)

    x = x_ref[...]
    xb = x.astype(_BF16)

    def up(c0):
        return _dot(xb, wup_ref[:, c0:c0 + fc])

    def conv(h, col0):
        cols = slice(col0, col0 + fc)
        prev8 = carry[:, cols]
        carry[:, cols] = h[tm - SUBLANES:tm, :]
        return (cb_ref[:, cols] + cw_ref[2:3, cols] * h
                + cw_ref[1:2, cols] * _shift_rows(h, prev8, 1)
                + cw_ref[0:1, cols] * _shift_rows(h, prev8, 2))

    n_chunks = D_FF // fc
    acc = jnp.zeros((tm, D_MODEL), _F32)
    h_gate, h_lin = up(0), up(D_FF)
    act = None
    for j in range(n_chunks):
        if j + 1 < n_chunks:
            next_gate, next_lin = up((j + 1) * fc), up(D_FF + (j + 1) * fc)
        if act is not None:
            acc = acc + _dot(act, wdn_ref[(j - 1) * fc:j * fc, :])
        act = (_gelu_x2(conv(h_gate, j * fc)) * conv(h_lin, D_FF + j * fc)).astype(_BF16)
        h_gate, h_lin = next_gate, next_lin
    acc = acc + _dot(act, wdn_ref[(n_chunks - 1) * fc:, :])
    o_ref[...] = _layer_norm(DN_ALPHA * x + acc, g_ref[...], b_ref[...])


def _ffn_layer(x2d, batch, seq, wup, cw, cb, wdn, g, b):
    tm = TOKEN_BLOCK
    ns = seq // tm
    tok = pl.BlockSpec((tm, D_MODEL), lambda bi, si: (bi * ns + si, 0))
    return pl.pallas_call(
        _ffn_kernel,
        grid=(batch, ns),
        in_specs=[tok, _const_spec(wup.shape), _const_spec(cw.shape), _const_spec(cb.shape),
                  _const_spec(wdn.shape), _const_spec(g.shape), _const_spec(b.shape)],
        out_specs=tok,
        out_shape=jax.ShapeDtypeStruct(x2d.shape, _F32),
        scratch_shapes=[pltpu.VMEM((SUBLANES, 2 * D_FF), _F32)],
        compiler_params=_params(("arbitrary", "arbitrary")),
        name="conv_ffn",
    )(x2d, wup, cw, cb, wdn, g, b)


def _qkv_kernel(x_ref, w_ref, o_ref):
    res = _dot(x_ref[...].astype(_BF16), w_ref[...])
    for n in range(3 * N_HEADS):
        part = res[:, n * HEAD_DIM:(n + 1) * HEAD_DIM]
        if n >= 2 * N_HEADS:
            part = part * (HEAD_DIM ** -0.5)
        o_ref[n] = part.astype(_BF16)


def _qkv_proj(x2d, w):
    tm = TOKEN_BLOCK
    tokens = x2d.shape[0]
    return pl.pallas_call(
        _qkv_kernel,
        grid=(tokens // tm,),
        in_specs=[pl.BlockSpec((tm, D_MODEL), lambda i: (i, 0)), _const_spec(w.shape)],
        out_specs=pl.BlockSpec((3 * N_HEADS, tm, HEAD_DIM), lambda i: (0, i, 0)),
        out_shape=jax.ShapeDtypeStruct((3 * N_HEADS, tokens, HEAD_DIM), _BF16),
        compiler_params=_params(("arbitrary",)),
        name="kvq_proj",
    )(x2d, w)


def _attn_kernel(q_ref, k_ref, v_ref, o_ref, o_acc, tail_acc):
    tq, tk = ATTN_Q_BLOCK, ATTN_K_BLOCK
    i = pl.program_id(1)
    row = lax.broadcasted_iota(jnp.int32, (tq, tk), 0)
    col = lax.broadcasted_iota(jnp.int32, (tq, tk), 1)
    later = (row > col).astype(_BF16)
    causal = col < row

    def logits(h, j):
        k = k_ref[h, pl.ds(pl.multiple_of(j * tk, tk), tk), :]
        z = lax.dot_general(q_ref[h], k, (((1,), (1,)), ((), ())), preferred_element_type=_F32)
        log_beta = jnp.minimum(z, 0.0) - jnp.log(1.0 + jnp.exp(-jnp.abs(z)))
        return log_beta, log_beta - z

    def values(h, j):
        return v_ref[h, pl.ds(pl.multiple_of(j * tk, tk), tk), :]

    def earlier(h, j, o, tail):
        log_beta, log_1m = logits(h, j)
        suffix = _dot(log_1m.astype(_BF16), later) + tail
        w = jnp.exp(log_beta + suffix)
        return (o + _dot(w.astype(_BF16), values(h, j)),
                tail + jnp.sum(log_1m, axis=1, keepdims=True))

    def stage_logits(h, j, on_diagonal):
        log_beta, log_1m = logits(h, j)
        if on_diagonal:
            log_1m = jnp.where(causal, log_1m, 0.0)
        return log_beta, log_1m.astype(_BF16), jnp.sum(log_1m, axis=1, keepdims=True)

    def stage_weights(h, j, on_diagonal, log_beta, suffix, tail):
        x = log_beta + suffix if tail is None else log_beta + suffix + tail
        w = jnp.exp(x)
        if on_diagonal:
            w = jnp.where(causal, w, 0.0)
        return _dot(w.astype(_BF16), values(h, j))

    def run_tiles(tiles):
        first, suffix, out = {}, {}, {}
        for t in range(len(tiles) + 2):
            if t < len(tiles):
                h, j, diag, _ = tiles[t]
                first[t] = stage_logits(h, j, diag)
            if 0 <= t - 1 < len(tiles):
                suffix[t - 1] = _dot(first[t - 1][1], later)
            if 0 <= t - 2 < len(tiles):
                h, j, diag, prev = tiles[t - 2]
                tail = None if prev is None else first[prev][2]
                out[t - 2] = stage_weights(h, j, diag, first[t - 2][0], suffix[t - 2], tail)
        return [out[t] for t in range(len(tiles))], [first[t][2] for t in range(len(tiles))]

    def alive(tail):
        return jnp.max(tail) >= EXP_ZERO_CUTOFF

    def emit(h, o):
        o_ref[:, h * HEAD_DIM:(h + 1) * HEAD_DIM] = o.astype(o_ref.dtype)

    @pl.when(i == 0)
    def _():
        outs, _ = run_tiles([(h, i, True, None) for h in range(N_HEADS)])
        for h in range(N_HEADS):
            emit(h, outs[h])

    @pl.when(i > 0)
    def _():
        tiles = []
        for h in range(N_HEADS):
            tiles += [(h, i, True, None), (h, i - 1, False, 2 * h)]
        outs, sums = run_tiles(tiles)
        worst = jnp.full((tq, 1), -jnp.inf, _F32)
        for h in range(N_HEADS):
            tail = sums[2 * h] + sums[2 * h + 1]
            o_acc[h] = outs[2 * h] + outs[2 * h + 1]
            tail_acc[h] = tail
            worst = jnp.maximum(worst, tail)

        @pl.when(jnp.logical_and(i > 1, alive(worst)))
        def _():
            def head(h, _):
                def cond(state):
                    j, _, _, live = state
                    return jnp.logical_and(j >= 0, live > 0)

                def body(state):
                    j, o, tail, _ = state
                    o, tail = earlier(h, j, o, tail)
                    return j - 1, o, tail, alive(tail).astype(jnp.int32)

                tail = tail_acc[h]
                state = (i - 2, o_acc[h], tail, alive(tail).astype(jnp.int32))
                o_acc[h] = lax.while_loop(cond, body, state)[1]
                return 0

            lax.fori_loop(0, N_HEADS, head, 0)

        for h in range(N_HEADS):
            emit(h, o_acc[h])


def _attention(kvq, batch, seq):
    tq = ATTN_Q_BLOCK
    nq = seq // tq
    return pl.pallas_call(
        _attn_kernel,
        grid=(batch, nq),
        in_specs=[pl.BlockSpec((N_HEADS, tq, HEAD_DIM), lambda b, i: (2, b * nq + i, 0)),
                  pl.BlockSpec((N_HEADS, seq, HEAD_DIM), lambda b, i: (0, b, 0)),
                  pl.BlockSpec((N_HEADS, seq, HEAD_DIM), lambda b, i: (1, b, 0))],
        out_specs=pl.BlockSpec((tq, N_HEADS * HEAD_DIM), lambda b, i: (b * nq + i, 0)),
        out_shape=jax.ShapeDtypeStruct((batch * seq, N_HEADS * HEAD_DIM), _BF16),
        scratch_shapes=[pltpu.VMEM((N_HEADS, tq, HEAD_DIM), _F32),
                        pltpu.VMEM((N_HEADS, tq, 1), _F32)],
        compiler_params=_params(("arbitrary", "arbitrary")),
        name="stickbreak_attn",
    )(kvq, kvq, kvq)


def _proj_ln_kernel(o_ref, x_ref, w_ref, g_ref, b_ref, y_ref):
    mix = _dot(o_ref[...], w_ref[...])
    y_ref[...] = _layer_norm(DN_ALPHA * x_ref[...] + mix, g_ref[...], b_ref[...])


def _proj_ln(o2d, x2d, w, g, b):
    tm = TOKEN_BLOCK
    tok = lambda i: (i, 0)
    return pl.pallas_call(
        _proj_ln_kernel,
        grid=(x2d.shape[0] // tm,),
        in_specs=[pl.BlockSpec((tm, o2d.shape[1]), tok), pl.BlockSpec((tm, D_MODEL), tok),
                  _const_spec(w.shape), _const_spec(g.shape), _const_spec(b.shape)],
        out_specs=pl.BlockSpec((tm, D_MODEL), tok),
        out_shape=jax.ShapeDtypeStruct(x2d.shape, _F32),
        compiler_params=_params(("arbitrary",)),
        name="attn_out_ln",
    )(o2d, x2d, w, g, b)


def kernel(x, lru_w_in, lru_b_in, lru_conv_w, lru_conv_b, lru_w_gates, lru_b_gates, lru_a_param, lru_w_out, lru_b_out, kv_w, attn_w_q, attn_w_out, ffn_w_up, ffn_conv_w, ffn_conv_b, ffn_w_down, ln_g, ln_b):
    batch, seq, d = x.shape
    assert d == D_MODEL and seq % TOKEN_BLOCK == 0 and seq % ATTN_Q_BLOCK == 0
    assert lru_w_in.shape[0] == 1 and attn_w_q.shape[0] == 1 and ffn_w_up.shape[0] == DEPTH
    row = lambda v: v.reshape(1, -1)
    h = x.reshape(batch * seq, d)

    half_lin = jnp.concatenate([jnp.ones((D_FF,), _F32), jnp.full((D_FF,), 0.5, _F32)])

    def ffn(h, layer):
        return _ffn_layer(h, batch, seq, ffn_w_up[layer].astype(_BF16), ffn_conv_w[layer] * half_lin,
                          row(ffn_conv_b[layer] * half_lin), ffn_w_down[layer].astype(_BF16),
                          row(ln_g[layer, 1]), row(ln_b[layer, 1]))

    h = _lru_layer(h, batch, seq, lru_w_in[0].astype(_BF16), row(lru_b_in[0]), lru_conv_w[0],
                   row(lru_conv_b[0]), (0.5 * lru_w_gates[0]).astype(_BF16),
                   (0.5 * lru_b_gates[0]).reshape(LRU_HEADS, 1, 2 * LRU_BLOCK), row(lru_a_param[0]),
                   (0.5 * lru_w_out[0]).astype(_BF16), row(lru_b_out[0]), row(ln_g[0, 0]), row(ln_b[0, 0]))
    h = ffn(h, 0)
    kvq = _qkv_proj(h, jnp.concatenate([kv_w, attn_w_q[0]], axis=1).astype(_BF16))
    o = _attention(kvq, batch, seq)
    h = _proj_ln(o, h, attn_w_out[0].astype(_BF16), row(ln_g[1, 0]), row(ln_b[1, 0]))
    h = ffn(h, 1)
    return h.reshape(batch, seq, d)
```

```python
import functools
import math

import jax
import jax.numpy as jnp
from jax import lax
from jax.experimental import pallas as pl
from jax.experimental.pallas import tpu as pltpu

D_MODEL = 1024
DEPTH = 2
LRU_WIDTH = D_MODEL
LRU_HEADS = 8
LRU_BLOCK = LRU_WIDTH // LRU_HEADS
LRU_CONV = 4
LRU_C = 8.0
N_HEADS = 8
HEAD_DIM = D_MODEL // N_HEADS
D_FF = 3 * D_MODEL
FFN_CONV = 3
DN_ALPHA = (2 * DEPTH) ** 0.25
LN_EPS = 1e-5

SUBLANES = 8
TOKEN_BLOCK = 512
FF_CHUNK = 512
ATTN_Q_BLOCK = 256
ATTN_K_BLOCK = 256
EXP_ZERO_CUTOFF = -104.0
VMEM_LIMIT = 52 * 1024 * 1024

_BF16 = jnp.bfloat16
_F32 = jnp.float32


def _dot(a, b):
    return jnp.dot(a, b, preferred_element_type=_F32)


def _gelu_x2(x):
    c = math.sqrt(2.0 / math.pi)
    t = jnp.tanh(x * (c + (c * 0.044715) * (x * x)))
    return x + x * t


def _layer_norm(y, g, b):
    mu = jnp.mean(y, axis=-1, keepdims=True)
    yc = y - mu
    var = jnp.mean(yc * yc, axis=-1, keepdims=True)
    return yc * lax.rsqrt(var + LN_EPS) * g + b


def _const_spec(shape):
    zeros = (0,) * len(shape)
    return pl.BlockSpec(shape, lambda *_: zeros, pipeline_mode=pl.Buffered(1))


def _params(sem):
    return pltpu.CompilerParams(dimension_semantics=sem, vmem_limit_bytes=VMEM_LIMIT)


def _lru_kernel(x_ref, win_ref, bin_ref, cw_ref, cb_ref, wg_ref, bg_ref, ap_ref,
                wout_ref, bout_ref, g_ref, b_ref, o_ref,
                xp_buf, x_tail, a_buf, u_buf, h_buf, h_carry):
    tm = x_ref.shape[0]
    W = LRU_WIDTH
    s = pl.program_id(1)

    @pl.when(s == 0)
    def _():
        x_tail[...] = jnp.zeros(x_tail.shape, _F32)
        h_carry[...] = jnp.zeros((SUBLANES, W), _F32)

    hm = tm // 2
    nt = 2 * LRU_BLOCK
    half_c_sp = (-0.5 * LRU_C) * jnp.logaddexp(-ap_ref[...], 0.0)
    sub = lax.broadcasted_iota(jnp.int32, (SUBLANES, W), 0)
    row = lax.broadcasted_iota(jnp.int32, (hm, LRU_BLOCK), 0)

    def in_proj_tile(k, t):
        xk = x_ref[k * hm:(k + 1) * hm, :].astype(_BF16)
        return _dot(xk, win_ref[:, t * nt:(t + 1) * nt]) + bin_ref[:, t * nt:(t + 1) * nt]

    seg = hm // SUBLANES
    per_seg = seg // SUBLANES

    def permuted_start(g):
        return SUBLANES * SUBLANES * (g % per_seg) + g // per_seg

    def recurrence(k, proj, h_prev, side_jobs):
        results = []
        first_rows = lax.broadcasted_iota(jnp.int32, (SUBLANES, LRU_BLOCK), 0) < 1
        for n in range(LRU_HEADS):
            results.append(side_jobs[n]())
            cols = slice(n * LRU_BLOCK, (n + 1) * LRU_BLOCK)
            tile = proj[W // nt + n // 2]
            xr = tile[:, (n % 2) * LRU_BLOCK:(n % 2 + 1) * LRU_BLOCK]
            for g in range(hm // SUBLANES):
                xp_buf[n, pl.ds(permuted_start(g), SUBLANES, stride=SUBLANES), :] = (
                    xr[g * SUBLANES:(g + 1) * SUBLANES, :])
            xp = xp_buf[n]
            prev = x_tail[n]
            keep = (LRU_CONV - 1) * SUBLANES
            x_tail[n] = xp[hm - keep:, :]
            wrap = [jnp.where(first_rows,
                              pltpu.roll(prev[d * SUBLANES:(d + 1) * SUBLANES, :], 1, 0),
                              pltpu.roll(xp[hm - keep + d * SUBLANES:hm - keep + (d + 1) * SUBLANES, :], 1, 0))
                    for d in range(LRU_CONV - 1)]
            xn = cb_ref[:, cols] + cw_ref[LRU_CONV - 1:LRU_CONV, cols] * xp
            for d in range(1, LRU_CONV):
                back = jnp.concatenate(wrap[LRU_CONV - 1 - d:] + [xp[:hm - d * SUBLANES, :]], axis=0)
                xn = xn + cw_ref[LRU_CONV - 1 - d:LRU_CONV - d, cols] * back
            t = jnp.tanh(_dot(xn.astype(_BF16), wg_ref[n]) + bg_ref[n])
            gate_i = 0.5 + 0.5 * t[:, :LRU_BLOCK]
            a = jnp.exp(half_c_sp[:, cols] + half_c_sp[:, cols] * t[:, LRU_BLOCK:])
            mult = jnp.sqrt(jnp.maximum(1.0 - a * a, 0.0))
            if k == 0:
                mult = jnp.where(jnp.logical_and(row == 0, s == 0), 1.0, mult)
            a_buf[k * hm:(k + 1) * hm, cols] = a
            u_buf[k * hm:(k + 1) * hm, cols] = mult * gate_i * xn

        def load(v):
            r = pl.multiple_of(k * hm + v * SUBLANES, SUBLANES)
            return a_buf[pl.ds(r, SUBLANES), :], u_buf[pl.ds(r, SUBLANES), :], r

        def pass1(v, state):
            h, p = state
            a, u, _ = load(v)
            return a * h + u, p * a

        end, prod = lax.fori_loop(0, seg, pass1, (jnp.zeros((SUBLANES, W), _F32),
                                                  jnp.ones((SUBLANES, W), _F32)))
        for d in (1, 2, 4):
            ok = sub >= d
            end = end + prod * jnp.where(ok, pltpu.roll(end, d, 0), 0.0)
            prod = prod * jnp.where(ok, pltpu.roll(prod, d, 0), 1.0)
        seg_out = end + prod * h_prev
        seg_in = jnp.where(sub < 1, h_prev, pltpu.roll(seg_out, 1, 0))

        def pass2(v, h):
            a, u, r = load(v)
            h = a * h + u
            for n in range(LRU_HEADS):
                h_buf[n, pl.ds(r, SUBLANES), :] = h[:, n * LRU_BLOCK:(n + 1) * LRU_BLOCK]
            return h

        lax.fori_loop(0, seg, pass2, seg_in)
        return jnp.broadcast_to(seg_out[SUBLANES - 1:SUBLANES, :], (SUBLANES, W)), results

    def gated(k, proj):
        y_br = _gelu_x2(jnp.concatenate(proj[:W // nt], axis=1))
        h = jnp.concatenate(
            [jnp.concatenate(
                [h_buf[n, pl.ds(k * hm + permuted_start(g), SUBLANES, stride=SUBLANES), :]
                 for g in range(hm // SUBLANES)], axis=0)
             for n in range(LRU_HEADS)], axis=1)
        return (h * y_br).astype(_BF16)

    def out_proj_tile(hy, t):
        return _dot(hy, wout_ref[:, t * nt:(t + 1) * nt]) + bout_ref[:, t * nt:(t + 1) * nt]

    def finish(k, mix_tiles):
        rows = slice(k * hm, (k + 1) * hm)
        mix = jnp.concatenate(mix_tiles, axis=1)
        o_ref[rows, :] = _layer_norm(DN_ALPHA * x_ref[rows, :] + mix, g_ref[...], b_ref[...])

    n_tiles = 2 * W // nt
    idle = lambda: None
    proj0 = [in_proj_tile(0, t) for t in range(n_tiles)]
    h_mid, proj1 = recurrence(0, proj0, h_carry[...],
                              [functools.partial(in_proj_tile, 1, t) for t in range(n_tiles)])
    hy0 = gated(0, proj0)
    out_jobs = [functools.partial(out_proj_tile, hy0, t) for t in range(D_MODEL // nt)]
    h_last, mix0 = recurrence(1, proj1, h_mid, out_jobs + [idle] * (LRU_HEADS - len(out_jobs)))
    h_carry[...] = h_last
    finish(0, mix0[:len(out_jobs)])
    hy1 = gated(1, proj1)
    finish(1, [out_proj_tile(hy1, t) for t in range(D_MODEL // nt)])


def _lru_layer(x2d, batch, seq, win, b_in, cw, cb, wg, bg, ap, wout, bout, g, b):
    tm = TOKEN_BLOCK
    W = LRU_WIDTH
    ns = seq // tm
    tok = pl.BlockSpec((tm, D_MODEL), lambda bi, si: (bi * ns + si, 0))
    return pl.pallas_call(
        _lru_kernel,
        grid=(batch, ns),
        in_specs=[tok, _const_spec(win.shape), _const_spec(b_in.shape), _const_spec(cw.shape),
                  _const_spec(cb.shape), _const_spec(wg.shape), _const_spec(bg.shape),
                  _const_spec(ap.shape), _const_spec(wout.shape), _const_spec(bout.shape),
                  _const_spec(g.shape), _const_spec(b.shape)],
        out_specs=tok,
        out_shape=jax.ShapeDtypeStruct(x2d.shape, _F32),
        scratch_shapes=[pltpu.VMEM((LRU_HEADS, tm // 2, LRU_BLOCK), _F32),
                        pltpu.VMEM((LRU_HEADS, (LRU_CONV - 1) * SUBLANES, LRU_BLOCK), _F32),
                        pltpu.VMEM((tm, W), _F32), pltpu.VMEM((tm, W), _F32),
                        pltpu.VMEM((LRU_HEADS, tm, LRU_BLOCK), _F32),
                        pltpu.VMEM((SUBLANES, W), _F32)],
        compiler_params=_params(("arbitrary", "arbitrary")),
        name="lru_layer",
    )(x2d, win, b_in, cw, cb, wg, bg, ap, wout, bout, g, b)


def _shift_rows(h, prev8, d):
    sub = lax.broadcasted_iota(jnp.int32, prev8.shape, 0)
    rolled = pltpu.roll(h, d, 0)
    top = jnp.where(sub < d, pltpu.roll(prev8, d, 0), rolled[0:SUBLANES])
    return jnp.concatenate([top, rolled[SUBLANES:]], axis=0)


def _ffn_kernel(x_ref, wup_ref, cw_ref, cb_ref, wdn_ref, g_ref, b_ref, o_ref, carry):
    tm = x_ref.shape[0]
    fc = FF_CHUNK
    s = pl.program_id(1)

    @pl.when(s == 0)
    def _():
        carry[...] = jnp.zeros(carry.shape, _F32)

    x = x_ref[...]
    xb = x.astype(_BF16)

    def up(c0):
        return _dot(xb, wup_ref[:, c0:c0 + fc])

    def conv(h, col0):
        cols = slice(col0, col0 + fc)
        prev8 = carry[:, cols]
        carry[:, cols] = h[tm - SUBLANES:tm, :]
        return (cb_ref[:, cols] + cw_ref[2:3, cols] * h
                + cw_ref[1:2, cols] * _shift_rows(h, prev8, 1)
                + cw_ref[0:1, cols] * _shift_rows(h, prev8, 2))

    n_chunks = D_FF // fc
    acc = jnp.zeros((tm, D_MODEL), _F32)
    h_gate, h_lin = up(0), up(D_FF)
    act = None
    for j in range(n_chunks):
        if j + 1 < n_chunks:
            next_gate, next_lin = up((j + 1) * fc), up(D_FF + (j + 1) * fc)
        if act is not None:
            acc = acc + _dot(act, wdn_ref[(j - 1) * fc:j * fc, :])
        act = (_gelu_x2(conv(h_gate, j * fc)) * conv(h_lin, D_FF + j * fc)).astype(_BF16)
        h_gate, h_lin = next_gate, next_lin
    acc = acc + _dot(act, wdn_ref[(n_chunks - 1) * fc:, :])
    o_ref[...] = _layer_norm(DN_ALPHA * x + acc, g_ref[...], b_ref[...])


def _ffn_layer(x2d, batch, seq, wup, cw, cb, wdn, g, b):
    tm = TOKEN_BLOCK
    ns = seq // tm
    tok = pl.BlockSpec((tm, D_MODEL), lambda bi, si: (bi * ns + si, 0))
    return pl.pallas_call(
        _ffn_kernel,
        grid=(batch, ns),
        in_specs=[tok, _const_spec(wup.shape), _const_spec(cw.shape), _const_spec(cb.shape),
                  _const_spec(wdn.shape), _const_spec(g.shape), _const_spec(b.shape)],
        out_specs=tok,
        out_shape=jax.ShapeDtypeStruct(x2d.shape, _F32),
        scratch_shapes=[pltpu.VMEM((SUBLANES, 2 * D_FF), _F32)],
        compiler_params=_params(("arbitrary", "arbitrary")),
        name="conv_ffn",
    )(x2d, wup, cw, cb, wdn, g, b)


def _qkv_kernel(x_ref, w_ref, o_ref):
    res = _dot(x_ref[...].astype(_BF16), w_ref[...])
    for n in range(3 * N_HEADS):
        part = res[:, n * HEAD_DIM:(n + 1) * HEAD_DIM]
        if n >= 2 * N_HEADS:
            part = part * (HEAD_DIM ** -0.5)
        o_ref[n] = part.astype(_BF16)


def _qkv_proj(x2d, w):
    tm = TOKEN_BLOCK
    tokens = x2d.shape[0]
    return pl.pallas_call(
        _qkv_kernel,
        grid=(tokens // tm,),
        in_specs=[pl.BlockSpec((tm, D_MODEL), lambda i: (i, 0)), _const_spec(w.shape)],
        out_specs=pl.BlockSpec((3 * N_HEADS, tm, HEAD_DIM), lambda i: (0, i, 0)),
        out_shape=jax.ShapeDtypeStruct((3 * N_HEADS, tokens, HEAD_DIM), _BF16),
        compiler_params=_params(("arbitrary",)),
        name="kvq_proj",
    )(x2d, w)


def _attn_kernel(q_ref, k_ref, v_ref, x_ref, wo_ref, g_ref, b_ref, y_ref, o_acc, tail_acc):
    tq, tk = ATTN_Q_BLOCK, ATTN_K_BLOCK
    i = pl.program_id(1)
    row = lax.broadcasted_iota(jnp.int32, (tq, tk), 0)
    col = lax.broadcasted_iota(jnp.int32, (tq, tk), 1)
    later = (row > col).astype(_BF16)
    causal = col < row

    def logits(h, j):
        k = k_ref[h, pl.ds(pl.multiple_of(j * tk, tk), tk), :]
        z = lax.dot_general(q_ref[h], k, (((1,), (1,)), ((), ())), preferred_element_type=_F32)
        log_beta = jnp.minimum(z, 0.0) - jnp.log(1.0 + jnp.exp(-jnp.abs(z)))
        return log_beta, log_beta - z

    def values(h, j):
        return v_ref[h, pl.ds(pl.multiple_of(j * tk, tk), tk), :]

    def earlier(h, j, o, tail):
        log_beta, log_1m = logits(h, j)
        suffix = _dot(log_1m.astype(_BF16), later) + tail
        w = jnp.exp(log_beta + suffix)
        return (o + _dot(w.astype(_BF16), values(h, j)),
                tail + jnp.sum(log_1m, axis=1, keepdims=True))

    def stage_logits(h, j, on_diagonal):
        log_beta, log_1m = logits(h, j)
        if on_diagonal:
            log_1m = jnp.where(causal, log_1m, 0.0)
        return log_beta, log_1m.astype(_BF16), jnp.sum(log_1m, axis=1, keepdims=True)

    def stage_weights(h, j, on_diagonal, log_beta, suffix, tail):
        x = log_beta + suffix if tail is None else log_beta + suffix + tail
        w = jnp.exp(x)
        if on_diagonal:
            w = jnp.where(causal, w, 0.0)
        return _dot(w.astype(_BF16), values(h, j))

    def run_tiles(tiles):
        first, suffix, out = {}, {}, {}
        for t in range(len(tiles) + 2):
            if t < len(tiles):
                h, j, diag, _ = tiles[t]
                first[t] = stage_logits(h, j, diag)
            if 0 <= t - 1 < len(tiles):
                suffix[t - 1] = _dot(first[t - 1][1], later)
            if 0 <= t - 2 < len(tiles):
                h, j, diag, prev = tiles[t - 2]
                tail = None if prev is None else first[prev][2]
                out[t - 2] = stage_weights(h, j, diag, first[t - 2][0], suffix[t - 2], tail)
        return [out[t] for t in range(len(tiles))], [first[t][2] for t in range(len(tiles))]

    def alive(tail):
        return jnp.max(tail) >= EXP_ZERO_CUTOFF

    @pl.when(i == 0)
    def _():
        outs, _ = run_tiles([(h, i, True, None) for h in range(N_HEADS)])
        for h in range(N_HEADS):
            o_acc[h] = outs[h]

    @pl.when(i > 0)
    def _():
        tiles = []
        for h in range(N_HEADS):
            tiles += [(h, i, True, None), (h, i - 1, False, 2 * h)]
        outs, sums = run_tiles(tiles)
        worst = jnp.full((tq, 1), -jnp.inf, _F32)
        for h in range(N_HEADS):
            tail = sums[2 * h] + sums[2 * h + 1]
            o_acc[h] = outs[2 * h] + outs[2 * h + 1]
            tail_acc[h] = tail
            worst = jnp.maximum(worst, tail)

        @pl.when(jnp.logical_and(i > 1, alive(worst)))
        def _():
            def head(h, _):
                def cond(state):
                    j, _, _, live = state
                    return jnp.logical_and(j >= 0, live > 0)

                def body(state):
                    j, o, tail, _ = state
                    o, tail = earlier(h, j, o, tail)
                    return j - 1, o, tail, alive(tail).astype(jnp.int32)

                tail = tail_acc[h]
                state = (i - 2, o_acc[h], tail, alive(tail).astype(jnp.int32))
                o_acc[h] = lax.while_loop(cond, body, state)[1]
                return 0

            lax.fori_loop(0, N_HEADS, head, 0)

    o = jnp.concatenate([o_acc[h] for h in range(N_HEADS)], axis=1).astype(_BF16)
    y_ref[...] = _layer_norm(DN_ALPHA * x_ref[...] + _dot(o, wo_ref[...]), g_ref[...], b_ref[...])


def _attention_layer(kvq, x2d, batch, seq, wo, g, b):
    tq = ATTN_Q_BLOCK
    nq = seq // tq
    tok = pl.BlockSpec((tq, D_MODEL), lambda bi, i: (bi * nq + i, 0))
    return pl.pallas_call(
        _attn_kernel,
        grid=(batch, nq),
        in_specs=[pl.BlockSpec((N_HEADS, tq, HEAD_DIM), lambda bi, i: (2, bi * nq + i, 0)),
                  pl.BlockSpec((N_HEADS, seq, HEAD_DIM), lambda bi, i: (0, bi, 0)),
                  pl.BlockSpec((N_HEADS, seq, HEAD_DIM), lambda bi, i: (1, bi, 0)),
                  tok, _const_spec(wo.shape), _const_spec(g.shape), _const_spec(b.shape)],
        out_specs=tok,
        out_shape=jax.ShapeDtypeStruct(x2d.shape, _F32),
        scratch_shapes=[pltpu.VMEM((N_HEADS, tq, HEAD_DIM), _F32),
                        pltpu.VMEM((N_HEADS, tq, 1), _F32)],
        compiler_params=_params(("arbitrary", "arbitrary")),
        name="stickbreak_attn",
    )(kvq, kvq, kvq, x2d, wo, g, b)


def kernel(x, lru_w_in, lru_b_in, lru_conv_w, lru_conv_b, lru_w_gates, lru_b_gates, lru_a_param, lru_w_out, lru_b_out, kv_w, attn_w_q, attn_w_out, ffn_w_up, ffn_conv_w, ffn_conv_b, ffn_w_down, ln_g, ln_b):
    batch, seq, d = x.shape
    assert d == D_MODEL and seq % TOKEN_BLOCK == 0 and seq % ATTN_Q_BLOCK == 0
    assert lru_w_in.shape[0] == 1 and attn_w_q.shape[0] == 1 and ffn_w_up.shape[0] == DEPTH
    row = lambda v: v.reshape(1, -1)
    h = x.reshape(batch * seq, d)

    half_lin = jnp.concatenate([jnp.ones((D_FF,), _F32), jnp.full((D_FF,), 0.5, _F32)])

    def ffn(h, layer):
        return _ffn_layer(h, batch, seq, ffn_w_up[layer].astype(_BF16), ffn_conv_w[layer] * half_lin,
                          row(ffn_conv_b[layer] * half_lin), ffn_w_down[layer].astype(_BF16),
                          row(ln_g[layer, 1]), row(ln_b[layer, 1]))

    h = _lru_layer(h, batch, seq, lru_w_in[0].astype(_BF16), row(lru_b_in[0]), lru_conv_w[0],
                   row(lru_conv_b[0]), (0.5 * lru_w_gates[0]).astype(_BF16),
                   (0.5 * lru_b_gates[0]).reshape(LRU_HEADS, 1, 2 * LRU_BLOCK), row(lru_a_param[0]),
                   (0.5 * lru_w_out[0]).astype(_BF16), row(lru_b_out[0]), row(ln_g[0, 0]), row(ln_b[0, 0]))
    h = ffn(h, 0)
    kvq = _qkv_proj(h, jnp.concatenate([kv_w, attn_w_q[0]], axis=1).astype(_BF16))
    h = _attention_layer(kvq, h, batch, seq, attn_w_out[0].astype(_BF16),
                         row(ln_g[1, 0]), row(ln_b[1, 0]))
    h = ffn(h, 1)
    return h.reshape(batch, seq, d)
```

```python
import functools
import math

import jax
import jax.numpy as jnp
from jax import lax
from jax.experimental import pallas as pl
from jax.experimental.pallas import tpu as pltpu

D_MODEL = 1024
DEPTH = 2
LRU_WIDTH = D_MODEL
LRU_HEADS = 8
LRU_BLOCK = LRU_WIDTH // LRU_HEADS
LRU_CONV = 4
LRU_C = 8.0
N_HEADS = 8
HEAD_DIM = D_MODEL // N_HEADS
D_FF = 3 * D_MODEL
FFN_CONV = 3
DN_ALPHA = (2 * DEPTH) ** 0.25
LN_EPS = 1e-5

SUBLANES = 8
LANES = 128
TOKEN_BLOCK = 512
FF_CHUNK = 512
ATTN_Q_BLOCK = 256
ATTN_K_BLOCK = 256
EXP_ZERO_CUTOFF = -104.0
VMEM_LIMIT = 52 * 1024 * 1024

_BF16 = jnp.bfloat16
_F32 = jnp.float32


def _dot(a, b):
    return jnp.dot(a, b, preferred_element_type=_F32)


def _gelu_x2(x):
    c = math.sqrt(2.0 / math.pi)
    t = jnp.tanh(x * (c + (c * 0.044715) * (x * x)))
    return x + x * t


def _layer_norm(y, g, b):
    mu = jnp.mean(y, axis=-1, keepdims=True)
    yc = y - mu
    var = jnp.mean(yc * yc, axis=-1, keepdims=True)
    return yc * lax.rsqrt(var + LN_EPS) * g + b


def _const_spec(shape):
    zeros = (0,) * len(shape)
    return pl.BlockSpec(shape, lambda *_: zeros, pipeline_mode=pl.Buffered(1))


def _params(sem):
    return pltpu.CompilerParams(dimension_semantics=sem, vmem_limit_bytes=VMEM_LIMIT)


def _lru_kernel(x_ref, win_ref, bin_ref, cw_ref, cb_ref, wg_ref, bg_ref, ap_ref,
                wout_ref, bout_ref, g_ref, b_ref, o_ref,
                xp_buf, x_tail, a_buf, u_buf, h_buf, h_carry):
    tm = x_ref.shape[0]
    W = LRU_WIDTH
    s = pl.program_id(1)

    @pl.when(s == 0)
    def _():
        x_tail[...] = jnp.zeros(x_tail.shape, _F32)
        h_carry[...] = jnp.zeros((SUBLANES, W), _F32)

    hm = tm // 2
    nt = 2 * LRU_BLOCK
    half_c_sp = (-0.5 * LRU_C) * jnp.logaddexp(-ap_ref[...], 0.0)
    sub = lax.broadcasted_iota(jnp.int32, (SUBLANES, W), 0)
    row = lax.broadcasted_iota(jnp.int32, (hm, LRU_BLOCK), 0)

    def in_proj_tile(k, t):
        xk = x_ref[k * hm:(k + 1) * hm, :].astype(_BF16)
        return _dot(xk, win_ref[:, t * nt:(t + 1) * nt]) + bin_ref[:, t * nt:(t + 1) * nt]

    seg = hm // SUBLANES
    per_seg = seg // SUBLANES

    def permuted_start(g):
        return SUBLANES * SUBLANES * (g % per_seg) + g // per_seg

    def recurrence(k, proj, h_prev, side_jobs):
        results = []
        first_rows = lax.broadcasted_iota(jnp.int32, (SUBLANES, LRU_BLOCK), 0) < 1
        for n in range(LRU_HEADS):
            results.append(side_jobs[n]())
            cols = slice(n * LRU_BLOCK, (n + 1) * LRU_BLOCK)
            tile = proj[W // nt + n // 2]
            xr = tile[:, (n % 2) * LRU_BLOCK:(n % 2 + 1) * LRU_BLOCK]
            for g in range(hm // SUBLANES):
                xp_buf[n, pl.ds(permuted_start(g), SUBLANES, stride=SUBLANES), :] = (
                    xr[g * SUBLANES:(g + 1) * SUBLANES, :])
            xp = xp_buf[n]
            prev = x_tail[n]
            keep = (LRU_CONV - 1) * SUBLANES
            x_tail[n] = xp[hm - keep:, :]
            wrap = [jnp.where(first_rows,
                              pltpu.roll(prev[d * SUBLANES:(d + 1) * SUBLANES, :], 1, 0),
                              pltpu.roll(xp[hm - keep + d * SUBLANES:hm - keep + (d + 1) * SUBLANES, :], 1, 0))
                    for d in range(LRU_CONV - 1)]
            xn = cb_ref[:, cols] + cw_ref[LRU_CONV - 1:LRU_CONV, cols] * xp
            for d in range(1, LRU_CONV):
                back = jnp.concatenate(wrap[LRU_CONV - 1 - d:] + [xp[:hm - d * SUBLANES, :]], axis=0)
                xn = xn + cw_ref[LRU_CONV - 1 - d:LRU_CONV - d, cols] * back
            t = jnp.tanh(_dot(xn.astype(_BF16), wg_ref[n]) + bg_ref[n])
            gate_i = 0.5 + 0.5 * t[:, :LRU_BLOCK]
            a = jnp.exp(half_c_sp[:, cols] + half_c_sp[:, cols] * t[:, LRU_BLOCK:])
            mult = jnp.sqrt(jnp.maximum(1.0 - a * a, 0.0))
            if k == 0:
                mult = jnp.where(jnp.logical_and(row == 0, s == 0), 1.0, mult)
            a_buf[k * hm:(k + 1) * hm, cols] = a
            u_buf[k * hm:(k + 1) * hm, cols] = mult * gate_i * xn

        def load(v):
            r = pl.multiple_of(k * hm + v * SUBLANES, SUBLANES)
            return a_buf[pl.ds(r, SUBLANES), :], u_buf[pl.ds(r, SUBLANES), :], r

        def pass1(v, state):
            h, p = state
            a, u, _ = load(v)
            return a * h + u, p * a

        end, prod = lax.fori_loop(0, seg, pass1, (jnp.zeros((SUBLANES, W), _F32),
                                                  jnp.ones((SUBLANES, W), _F32)))
        for d in (1, 2, 4):
            ok = sub >= d
            end = end + prod * jnp.where(ok, pltpu.roll(end, d, 0), 0.0)
            prod = prod * jnp.where(ok, pltpu.roll(prod, d, 0), 1.0)
        seg_out = end + prod * h_prev
        seg_in = jnp.where(sub < 1, h_prev, pltpu.roll(seg_out, 1, 0))

        def pass2(v, h):
            a, u, r = load(v)
            h = a * h + u
            for n in range(LRU_HEADS):
                h_buf[n, pl.ds(r, SUBLANES), :] = h[:, n * LRU_BLOCK:(n + 1) * LRU_BLOCK]
            return h

        lax.fori_loop(0, seg, pass2, seg_in)
        return jnp.broadcast_to(seg_out[SUBLANES - 1:SUBLANES, :], (SUBLANES, W)), results

    def gated(k, proj):
        y_br = _gelu_x2(jnp.concatenate(proj[:W // nt], axis=1))
        h = jnp.concatenate(
            [jnp.concatenate(
                [h_buf[n, pl.ds(k * hm + permuted_start(g), SUBLANES, stride=SUBLANES), :]
                 for g in range(hm // SUBLANES)], axis=0)
             for n in range(LRU_HEADS)], axis=1)
        return (h * y_br).astype(_BF16)

    def out_proj_tile(hy, t):
        return _dot(hy, wout_ref[:, t * nt:(t + 1) * nt]) + bout_ref[:, t * nt:(t + 1) * nt]

    def finish(k, mix_tiles):
        rows = slice(k * hm, (k + 1) * hm)
        mix = jnp.concatenate(mix_tiles, axis=1)
        o_ref[rows, :] = _layer_norm(DN_ALPHA * x_ref[rows, :] + mix, g_ref[...], b_ref[...])

    n_tiles = 2 * W // nt
    idle = lambda: None
    proj0 = [in_proj_tile(0, t) for t in range(n_tiles)]
    h_mid, proj1 = recurrence(0, proj0, h_carry[...],
                              [functools.partial(in_proj_tile, 1, t) for t in range(n_tiles)])
    hy0 = gated(0, proj0)
    out_jobs = [functools.partial(out_proj_tile, hy0, t) for t in range(D_MODEL // nt)]
    h_last, mix0 = recurrence(1, proj1, h_mid, out_jobs + [idle] * (LRU_HEADS - len(out_jobs)))
    h_carry[...] = h_last
    finish(0, mix0[:len(out_jobs)])
    hy1 = gated(1, proj1)
    finish(1, [out_proj_tile(hy1, t) for t in range(D_MODEL // nt)])


def _lru_layer(x2d, batch, seq, win, b_in, cw, cb, wg, bg, ap, wout, bout, g, b):
    tm = TOKEN_BLOCK
    W = LRU_WIDTH
    ns = seq // tm
    tok = pl.BlockSpec((tm, D_MODEL), lambda bi, si: (bi * ns + si, 0))
    return pl.pallas_call(
        _lru_kernel,
        grid=(batch, ns),
        in_specs=[tok, _const_spec(win.shape), _const_spec(b_in.shape), _const_spec(cw.shape),
                  _const_spec(cb.shape), _const_spec(wg.shape), _const_spec(bg.shape),
                  _const_spec(ap.shape), _const_spec(wout.shape), _const_spec(bout.shape),
                  _const_spec(g.shape), _const_spec(b.shape)],
        out_specs=tok,
        out_shape=jax.ShapeDtypeStruct(x2d.shape, _F32),
        scratch_shapes=[pltpu.VMEM((LRU_HEADS, tm // 2, LRU_BLOCK), _F32),
                        pltpu.VMEM((LRU_HEADS, (LRU_CONV - 1) * SUBLANES, LRU_BLOCK), _F32),
                        pltpu.VMEM((tm, W), _F32), pltpu.VMEM((tm, W), _F32),
                        pltpu.VMEM((LRU_HEADS, tm, LRU_BLOCK), _F32),
                        pltpu.VMEM((SUBLANES, W), _F32)],
        compiler_params=_params(("arbitrary", "arbitrary")),
        name="lru_layer",
    )(x2d, win, b_in, cw, cb, wg, bg, ap, wout, bout, g, b)


def _ffn_kernel(x_ref, wup_ref, cw_ref, cb_ref, wdn_ref, g_ref, b_ref, o_ref, carry, relay):
    tm = x_ref.shape[0]
    fc = FF_CHUNK
    s = pl.program_id(1)
    n_groups = tm // SUBLANES
    lane_tiles = D_MODEL // LANES
    taps_back = FFN_CONV - 1

    @pl.when(s == 0)
    def _():
        carry[...] = jnp.zeros(carry.shape, _F32)

    per_seg = n_groups // SUBLANES

    def permuted_start(g):
        return SUBLANES * SUBLANES * (g % per_seg) + g // per_seg

    x_nat = x_ref[...]
    for n in range(lane_tiles):
        for g in range(n_groups):
            relay[n, pl.ds(permuted_start(g), SUBLANES, stride=SUBLANES), :] = (
                x_nat[g * SUBLANES:(g + 1) * SUBLANES, n * LANES:(n + 1) * LANES])
    x = jnp.concatenate([relay[n] for n in range(lane_tiles)], axis=1)
    xb = x.astype(_BF16)

    def up(c0):
        return _dot(xb, wup_ref[:, c0:c0 + fc])

    first_rows = lax.broadcasted_iota(jnp.int32, (SUBLANES, fc), 0) < 1

    def conv(h, col0):
        cols = slice(col0, col0 + fc)
        keep = taps_back * SUBLANES
        prev = carry[:, cols]
        carry[:, cols] = h[tm - keep:, :]
        wrap = [jnp.where(first_rows,
                          pltpu.roll(prev[d * SUBLANES:(d + 1) * SUBLANES, :], 1, 0),
                          pltpu.roll(h[tm - keep + d * SUBLANES:tm - keep + (d + 1) * SUBLANES, :], 1, 0))
                for d in range(taps_back)]
        out = cb_ref[:, cols] + cw_ref[taps_back:taps_back + 1, cols] * h
        for d in range(1, FFN_CONV):
            back = jnp.concatenate(wrap[taps_back - d:] + [h[:tm - d * SUBLANES, :]], axis=0)
            out = out + cw_ref[taps_back - d:taps_back - d + 1, cols] * back
        return out

    n_chunks = D_FF // fc
    acc = jnp.zeros((tm, D_MODEL), _F32)
    h_gate, h_lin = up(0), up(D_FF)
    act = None
    for j in range(n_chunks):
        if j + 1 < n_chunks:
            next_gate, next_lin = up((j + 1) * fc), up(D_FF + (j + 1) * fc)
        if act is not None:
            acc = acc + _dot(act, wdn_ref[(j - 1) * fc:j * fc, :])
        act = (_gelu_x2(conv(h_gate, j * fc)) * conv(h_lin, D_FF + j * fc)).astype(_BF16)
        h_gate, h_lin = next_gate, next_lin
    acc = acc + _dot(act, wdn_ref[(n_chunks - 1) * fc:, :])
    y = _layer_norm(DN_ALPHA * x + acc, g_ref[...], b_ref[...])
    for n in range(lane_tiles):
        relay[n] = y[:, n * LANES:(n + 1) * LANES]
    o_ref[...] = jnp.concatenate(
        [jnp.concatenate([relay[n, pl.ds(permuted_start(g), SUBLANES, stride=SUBLANES), :]
                          for g in range(n_groups)], axis=0)
         for n in range(lane_tiles)], axis=1)


def _ffn_layer(x2d, batch, seq, wup, cw, cb, wdn, g, b):
    tm = TOKEN_BLOCK
    ns = seq // tm
    tok = pl.BlockSpec((tm, D_MODEL), lambda bi, si: (bi * ns + si, 0))
    return pl.pallas_call(
        _ffn_kernel,
        grid=(batch, ns),
        in_specs=[tok, _const_spec(wup.shape), _const_spec(cw.shape), _const_spec(cb.shape),
                  _const_spec(wdn.shape), _const_spec(g.shape), _const_spec(b.shape)],
        out_specs=tok,
        out_shape=jax.ShapeDtypeStruct(x2d.shape, _F32),
        scratch_shapes=[pltpu.VMEM(((FFN_CONV - 1) * SUBLANES, 2 * D_FF), _F32),
                        pltpu.VMEM((D_MODEL // LANES, tm, LANES), _F32)],
        compiler_params=_params(("arbitrary", "arbitrary")),
        name="conv_ffn",
    )(x2d, wup, cw, cb, wdn, g, b)


def _qkv_kernel(x_ref, w_ref, o_ref):
    res = _dot(x_ref[...].astype(_BF16), w_ref[...])
    for n in range(3 * N_HEADS):
        part = res[:, n * HEAD_DIM:(n + 1) * HEAD_DIM]
        if n >= 2 * N_HEADS:
            part = part * (HEAD_DIM ** -0.5)
        o_ref[n] = part.astype(_BF16)


def _qkv_proj(x2d, w):
    tm = TOKEN_BLOCK
    tokens = x2d.shape[0]
    return pl.pallas_call(
        _qkv_kernel,
        grid=(tokens // tm,),
        in_specs=[pl.BlockSpec((tm, D_MODEL), lambda i: (i, 0)), _const_spec(w.shape)],
        out_specs=pl.BlockSpec((3 * N_HEADS, tm, HEAD_DIM), lambda i: (0, i, 0)),
        out_shape=jax.ShapeDtypeStruct((3 * N_HEADS, tokens, HEAD_DIM), _BF16),
        compiler_params=_params(("arbitrary",)),
        name="kvq_proj",
    )(x2d, w)


def _attn_kernel(q_ref, k_ref, v_ref, x_ref, wo_ref, g_ref, b_ref, y_ref, o_acc, tail_acc):
    tq, tk = ATTN_Q_BLOCK, ATTN_K_BLOCK
    i = pl.program_id(1)
    row = lax.broadcasted_iota(jnp.int32, (tq, tk), 0)
    col = lax.broadcasted_iota(jnp.int32, (tq, tk), 1)
    later = (row > col).astype(_BF16)
    causal = col < row

    def logits(h, j):
        k = k_ref[h, pl.ds(pl.multiple_of(j * tk, tk), tk), :]
        z = lax.dot_general(q_ref[h], k, (((1,), (1,)), ((), ())), preferred_element_type=_F32)
        log_beta = jnp.minimum(z, 0.0) - jnp.log(1.0 + jnp.exp(-jnp.abs(z)))
        return log_beta, log_beta - z

    def values(h, j):
        return v_ref[h, pl.ds(pl.multiple_of(j * tk, tk), tk), :]

    def earlier(h, j, o, tail):
        log_beta, log_1m = logits(h, j)
        suffix = _dot(log_1m.astype(_BF16), later) + tail
        w = jnp.exp(log_beta + suffix)
        return (o + _dot(w.astype(_BF16), values(h, j)),
                tail + jnp.sum(log_1m, axis=1, keepdims=True))

    def stage_logits(h, j, on_diagonal):
        log_beta, log_1m = logits(h, j)
        if on_diagonal:
            log_1m = jnp.where(causal, log_1m, 0.0)
        return log_beta, log_1m.astype(_BF16), jnp.sum(log_1m, axis=1, keepdims=True)

    def stage_weights(h, j, on_diagonal, log_beta, suffix, tail):
        x = log_beta + suffix if tail is None else log_beta + suffix + tail
        w = jnp.exp(x)
        if on_diagonal:
            w = jnp.where(causal, w, 0.0)
        return _dot(w.astype(_BF16), values(h, j))

    def run_tiles(tiles):
        first, suffix, out = {}, {}, {}
        for t in range(len(tiles) + 2):
            if t < len(tiles):
                h, j, diag, _ = tiles[t]
                first[t] = stage_logits(h, j, diag)
            if 0 <= t - 1 < len(tiles):
                suffix[t - 1] = _dot(first[t - 1][1], later)
            if 0 <= t - 2 < len(tiles):
                h, j, diag, prev = tiles[t - 2]
                tail = None if prev is None else first[prev][2]
                out[t - 2] = stage_weights(h, j, diag, first[t - 2][0], suffix[t - 2], tail)
        return [out[t] for t in range(len(tiles))], [first[t][2] for t in range(len(tiles))]

    def alive(tail):
        return jnp.max(tail) >= EXP_ZERO_CUTOFF

    @pl.when(i == 0)
    def _():
        outs, _ = run_tiles([(h, i, True, None) for h in range(N_HEADS)])
        for h in range(N_HEADS):
            o_acc[h] = outs[h]

    @pl.when(i > 0)
    def _():
        tiles = []
        for h in range(N_HEADS):
            tiles += [(h, i, True, None), (h, i - 1, False, 2 * h)]
        outs, sums = run_tiles(tiles)
        worst = jnp.full((tq, 1), -jnp.inf, _F32)
        for h in range(N_HEADS):
            tail = sums[2 * h] + sums[2 * h + 1]
            o_acc[h] = outs[2 * h] + outs[2 * h + 1]
            tail_acc[h] = tail
            worst = jnp.maximum(worst, tail)

        @pl.when(jnp.logical_and(i > 1, alive(worst)))
        def _():
            def head(h, _):
                def cond(state):
                    j, _, _, live = state
                    return jnp.logical_and(j >= 0, live > 0)

                def body(state):
                    j, o, tail, _ = state
                    o, tail = earlier(h, j, o, tail)
                    return j - 1, o, tail, alive(tail).astype(jnp.int32)

                tail = tail_acc[h]
                state = (i - 2, o_acc[h], tail, alive(tail).astype(jnp.int32))
                o_acc[h] = lax.while_loop(cond, body, state)[1]
                return 0

            lax.fori_loop(0, N_HEADS, head, 0)

    o = jnp.concatenate([o_acc[h] for h in range(N_HEADS)], axis=1).astype(_BF16)
    y_ref[...] = _layer_norm(DN_ALPHA * x_ref[...] + _dot(o, wo_ref[...]), g_ref[...], b_ref[...])


def _attention_layer(kvq, x2d, batch, seq, wo, g, b):
    tq = ATTN_Q_BLOCK
    nq = seq // tq
    tok = pl.BlockSpec((tq, D_MODEL), lambda bi, i: (bi * nq + i, 0))
    return pl.pallas_call(
        _attn_kernel,
        grid=(batch, nq),
        in_specs=[pl.BlockSpec((N_HEADS, tq, HEAD_DIM), lambda bi, i: (2, bi * nq + i, 0)),
                  pl.BlockSpec((N_HEADS, seq, HEAD_DIM), lambda bi, i: (0, bi, 0)),
                  pl.BlockSpec((N_HEADS, seq, HEAD_DIM), lambda bi, i: (1, bi, 0)),
                  tok, _const_spec(wo.shape), _const_spec(g.shape), _const_spec(b.shape)],
        out_specs=tok,
        out_shape=jax.ShapeDtypeStruct(x2d.shape, _F32),
        scratch_shapes=[pltpu.VMEM((N_HEADS, tq, HEAD_DIM), _F32),
                        pltpu.VMEM((N_HEADS, tq, 1), _F32)],
        compiler_params=_params(("arbitrary", "arbitrary")),
        name="stickbreak_attn",
    )(kvq, kvq, kvq, x2d, wo, g, b)


def kernel(x, lru_w_in, lru_b_in, lru_conv_w, lru_conv_b, lru_w_gates, lru_b_gates, lru_a_param, lru_w_out, lru_b_out, kv_w, attn_w_q, attn_w_out, ffn_w_up, ffn_conv_w, ffn_conv_b, ffn_w_down, ln_g, ln_b):
    batch, seq, d = x.shape
    assert d == D_MODEL and seq % TOKEN_BLOCK == 0 and seq % ATTN_Q_BLOCK == 0
    assert lru_w_in.shape[0] == 1 and attn_w_q.shape[0] == 1 and ffn_w_up.shape[0] == DEPTH
    row = lambda v: v.reshape(1, -1)
    h = x.reshape(batch * seq, d)

    half_lin = jnp.concatenate([jnp.ones((D_FF,), _F32), jnp.full((D_FF,), 0.5, _F32)])

    def ffn(h, layer):
        return _ffn_layer(h, batch, seq, ffn_w_up[layer].astype(_BF16), ffn_conv_w[layer] * half_lin,
                          row(ffn_conv_b[layer] * half_lin), ffn_w_down[layer].astype(_BF16),
                          row(ln_g[layer, 1]), row(ln_b[layer, 1]))

    h = _lru_layer(h, batch, seq, lru_w_in[0].astype(_BF16), row(lru_b_in[0]), lru_conv_w[0],
                   row(lru_conv_b[0]), (0.5 * lru_w_gates[0]).astype(_BF16),
                   (0.5 * lru_b_gates[0]).reshape(LRU_HEADS, 1, 2 * LRU_BLOCK), row(lru_a_param[0]),
                   (0.5 * lru_w_out[0]).astype(_BF16), row(lru_b_out[0]), row(ln_g[0, 0]), row(ln_b[0, 0]))
    h = ffn(h, 0)
    kvq = _qkv_proj(h, jnp.concatenate([kv_w, attn_w_q[0]], axis=1).astype(_BF16))
    h = _attention_layer(kvq, h, batch, seq, attn_w_out[0].astype(_BF16),
                         row(ln_g[1, 0]), row(ln_b[1, 0]))
    h = ffn(h, 1)
    return h.reshape(batch, seq, d)
```

```python
import functools
import math

import jax
import jax.numpy as jnp
from jax import lax
from jax.experimental import pallas as pl
from jax.experimental.pallas import tpu as pltpu

D_MODEL = 1024
DEPTH = 2
LRU_WIDTH = D_MODEL
LRU_HEADS = 8
LRU_BLOCK = LRU_WIDTH // LRU_HEADS
LRU_CONV = 4
LRU_C = 8.0
N_HEADS = 8
HEAD_DIM = D_MODEL // N_HEADS
D_FF = 3 * D_MODEL
FFN_CONV = 3
DN_ALPHA = (2 * DEPTH) ** 0.25
LN_EPS = 1e-5

SUBLANES = 8
LANES = 128
TOKEN_BLOCK = 512
LRU_TOKEN_BLOCK = 1024
FF_CHUNK = 512
FF_TAIL_CHUNKS = [512]
ATTN_Q_BLOCK = 256
ATTN_K_BLOCK = 256
EXP_ZERO_CUTOFF = -104.0
VMEM_LIMIT = 52 * 1024 * 1024

_BF16 = jnp.bfloat16
_F32 = jnp.float32


def _dot(a, b):
    return jnp.dot(a, b, preferred_element_type=_F32)


def _gelu_x2(x):
    c = math.sqrt(2.0 / math.pi)
    t = jnp.tanh(x * (c + (c * 0.044715) * (x * x)))
    return x + x * t


def _layer_norm(y, g, b):
    mu = jnp.mean(y, axis=-1, keepdims=True)
    yc = y - mu
    var = jnp.mean(yc * yc, axis=-1, keepdims=True)
    return yc * lax.rsqrt(var + LN_EPS) * g + b


def _const_spec(shape):
    zeros = (0,) * len(shape)
    return pl.BlockSpec(shape, lambda *_: zeros, pipeline_mode=pl.Buffered(1))


def _params(sem):
    return pltpu.CompilerParams(dimension_semantics=sem, vmem_limit_bytes=VMEM_LIMIT)


def _lru_kernel(x_ref, win_ref, bin_ref, cw_ref, cb_ref, wg_ref, bg_ref, ap_ref,
                wout_ref, bout_ref, g_ref, b_ref, o_ref,
                xp_buf, x_tail, a_buf, u_buf, h_buf, h_carry):
    tm = x_ref.shape[0]
    W = LRU_WIDTH
    s = pl.program_id(1)

    @pl.when(s == 0)
    def _():
        x_tail[...] = jnp.zeros(x_tail.shape, _F32)
        h_carry[...] = jnp.zeros((SUBLANES, W), _F32)

    hm = tm // 2
    nt = 2 * LRU_BLOCK
    half_c_sp = (-0.5 * LRU_C) * jnp.logaddexp(-ap_ref[...], 0.0)
    sub = lax.broadcasted_iota(jnp.int32, (SUBLANES, W), 0)
    row = lax.broadcasted_iota(jnp.int32, (hm, LRU_BLOCK), 0)

    x_halves = [x_ref[k * hm:(k + 1) * hm, :].astype(_BF16) for k in range(2)]

    def in_proj_tile(k, t):
        return _dot(x_halves[k], win_ref[:, t * nt:(t + 1) * nt]) + bin_ref[:, t * nt:(t + 1) * nt]

    seg = hm // SUBLANES
    per_seg = seg // SUBLANES

    def permuted_start(g):
        return SUBLANES * SUBLANES * (g % per_seg) + g // per_seg

    def recurrence(k, xr_tiles, h_prev, side_jobs):
        results = []
        first_rows = lax.broadcasted_iota(jnp.int32, (SUBLANES, LRU_BLOCK), 0) < 1
        for n in range(LRU_HEADS):
            results += [job() for job in side_jobs[n]]
            cols = slice(n * LRU_BLOCK, (n + 1) * LRU_BLOCK)
            tile = xr_tiles[n // 2]
            xr = tile[:, (n % 2) * LRU_BLOCK:(n % 2 + 1) * LRU_BLOCK]
            for g in range(hm // SUBLANES):
                xp_buf[n, pl.ds(permuted_start(g), SUBLANES, stride=SUBLANES), :] = (
                    xr[g * SUBLANES:(g + 1) * SUBLANES, :])
            xp = xp_buf[n]
            prev = x_tail[n]
            keep = (LRU_CONV - 1) * SUBLANES
            x_tail[n] = xp[hm - keep:, :]
            wrap = [jnp.where(first_rows,
                              pltpu.roll(prev[d * SUBLANES:(d + 1) * SUBLANES, :], 1, 0),
                              pltpu.roll(xp[hm - keep + d * SUBLANES:hm - keep + (d + 1) * SUBLANES, :], 1, 0))
                    for d in range(LRU_CONV - 1)]
            xn = cb_ref[:, cols] + cw_ref[LRU_CONV - 1:LRU_CONV, cols] * xp
            for d in range(1, LRU_CONV):
                back = jnp.concatenate(wrap[LRU_CONV - 1 - d:] + [xp[:hm - d * SUBLANES, :]], axis=0)
                xn = xn + cw_ref[LRU_CONV - 1 - d:LRU_CONV - d, cols] * back
            t = jnp.tanh(_dot(xn.astype(_BF16), wg_ref[n]) + bg_ref[n])
            gate_i = 0.5 + 0.5 * t[:, :LRU_BLOCK]
            a = jnp.exp(half_c_sp[:, cols] + half_c_sp[:, cols] * t[:, LRU_BLOCK:])
            m = jnp.maximum(1.0 - a * a, 0.0)
            mult = jnp.where(m > 0.0, m * lax.rsqrt(m), 0.0)
            if k == 0:
                mult = jnp.where(jnp.logical_and(row == 0, s == 0), 1.0, mult)
            a_buf[k * hm:(k + 1) * hm, cols] = a
            u_buf[k * hm:(k + 1) * hm, cols] = mult * gate_i * xn

        def load(v):
            r = pl.multiple_of(k * hm + v * SUBLANES, SUBLANES)
            return a_buf[pl.ds(r, SUBLANES), :], u_buf[pl.ds(r, SUBLANES), :], r

        def pass1(v, state):
            h, p = state
            a, u, _ = load(v)
            return a * h + u, p * a

        end, prod = lax.fori_loop(0, seg, pass1, (jnp.zeros((SUBLANES, W), _F32),
                                                  jnp.ones((SUBLANES, W), _F32)))
        for d in (1, 2, 4):
            ok = sub >= d
            end = end + prod * jnp.where(ok, pltpu.roll(end, d, 0), 0.0)
            prod = prod * jnp.where(ok, pltpu.roll(prod, d, 0), 1.0)
        seg_out = end + prod * h_prev
        seg_in = jnp.where(sub < 1, h_prev, pltpu.roll(seg_out, 1, 0))

        def pass2(v, h):
            a, u, r = load(v)
            h = a * h + u
            for n in range(LRU_HEADS):
                h_buf[n, pl.ds(r, SUBLANES), :] = h[:, n * LRU_BLOCK:(n + 1) * LRU_BLOCK]
            return h

        lax.fori_loop(0, seg, pass2, seg_in)
        return jnp.broadcast_to(seg_out[SUBLANES - 1:SUBLANES, :], (SUBLANES, W)), results

    def gated(k, y_tiles):
        y_br = _gelu_x2(jnp.concatenate(y_tiles, axis=1))
        h = jnp.concatenate(
            [jnp.concatenate(
                [h_buf[n, pl.ds(k * hm + permuted_start(g), SUBLANES, stride=SUBLANES), :]
                 for g in range(hm // SUBLANES)], axis=0)
             for n in range(LRU_HEADS)], axis=1)
        return (h * y_br).astype(_BF16)

    def out_proj_tile(hy, t):
        return _dot(hy, wout_ref[:, t * nt:(t + 1) * nt]) + bout_ref[:, t * nt:(t + 1) * nt]

    def finish(k, mix_tiles):
        rows = slice(k * hm, (k + 1) * hm)
        mix = jnp.concatenate(mix_tiles, axis=1)
        o_ref[rows, :] = _layer_norm(DN_ALPHA * x_ref[rows, :] + mix, g_ref[...], b_ref[...])

    half = W // nt
    job = functools.partial
    xr0 = [in_proj_tile(0, half + t) for t in range(half)]
    jobs = ([[job(in_proj_tile, 0, t), job(in_proj_tile, 1, half + t)] for t in range(half)]
            + [[job(in_proj_tile, 1, t)] for t in range(half)])
    h_mid, res = recurrence(0, xr0, h_carry[...], jobs)
    y0, xr1, y1 = res[0:2 * half:2], res[1:2 * half:2], res[2 * half:]
    hy0 = gated(0, y0)
    n_out = D_MODEL // nt
    jobs = [[job(out_proj_tile, hy0, t)] for t in range(n_out)] + [[]] * (LRU_HEADS - n_out)
    h_last, mix0 = recurrence(1, xr1, h_mid, jobs)
    h_carry[...] = h_last
    finish(0, mix0)
    hy1 = gated(1, y1)
    finish(1, [out_proj_tile(hy1, t) for t in range(n_out)])


def _lru_layer(x2d, batch, seq, win, b_in, cw, cb, wg, bg, ap, wout, bout, g, b):
    tm = LRU_TOKEN_BLOCK
    W = LRU_WIDTH
    ns = seq // tm
    tok = pl.BlockSpec((tm, D_MODEL), lambda bi, si: (bi * ns + si, 0))
    return pl.pallas_call(
        _lru_kernel,
        grid=(batch, ns),
        in_specs=[tok, _const_spec(win.shape), _const_spec(b_in.shape), _const_spec(cw.shape),
                  _const_spec(cb.shape), _const_spec(wg.shape), _const_spec(bg.shape),
                  _const_spec(ap.shape), _const_spec(wout.shape), _const_spec(bout.shape),
                  _const_spec(g.shape), _const_spec(b.shape)],
        out_specs=tok,
        out_shape=jax.ShapeDtypeStruct(x2d.shape, _F32),
        scratch_shapes=[pltpu.VMEM((LRU_HEADS, tm // 2, LRU_BLOCK), _F32),
                        pltpu.VMEM((LRU_HEADS, (LRU_CONV - 1) * SUBLANES, LRU_BLOCK), _F32),
                        pltpu.VMEM((tm, W), _F32), pltpu.VMEM((tm, W), _F32),
                        pltpu.VMEM((LRU_HEADS, tm, LRU_BLOCK), _F32),
                        pltpu.VMEM((SUBLANES, W), _F32)],
        compiler_params=_params(("arbitrary", "arbitrary")),
        name="lru_layer",
    )(x2d, win, b_in, cw, cb, wg, bg, ap, wout, bout, g, b)


def _ffn_kernel(x_ref, wup_ref, cw_ref, cb_ref, wdn_ref, g_ref, b_ref, o_ref, carry, relay):
    tm = x_ref.shape[0]
    fc = FF_CHUNK
    s = pl.program_id(1)
    n_groups = tm // SUBLANES
    lane_tiles = D_MODEL // LANES
    taps_back = FFN_CONV - 1

    @pl.when(s == 0)
    def _():
        carry[...] = jnp.zeros(carry.shape, _F32)

    per_seg = n_groups // SUBLANES

    def permuted_start(g):
        return SUBLANES * SUBLANES * (g % per_seg) + g // per_seg

    x_nat = x_ref[...]
    for n in range(lane_tiles):
        for g in range(n_groups):
            relay[n, pl.ds(permuted_start(g), SUBLANES, stride=SUBLANES), :] = (
                x_nat[g * SUBLANES:(g + 1) * SUBLANES, n * LANES:(n + 1) * LANES])
    x = jnp.concatenate([relay[n] for n in range(lane_tiles)], axis=1)
    xb = x.astype(_BF16)

    def up(c0, width):
        return (_dot(xb, wup_ref[:, c0:c0 + width]),
                _dot(xb, wup_ref[:, D_FF + c0:D_FF + c0 + width]))

    def conv(h, col0):
        cols = slice(col0, col0 + h.shape[1])
        first_rows = lax.broadcasted_iota(jnp.int32, (SUBLANES, h.shape[1]), 0) < 1
        keep = taps_back * SUBLANES
        prev = carry[:, cols]
        carry[:, cols] = h[tm - keep:, :]
        wrap = [jnp.where(first_rows,
                          pltpu.roll(prev[d * SUBLANES:(d + 1) * SUBLANES, :], 1, 0),
                          pltpu.roll(h[tm - keep + d * SUBLANES:tm - keep + (d + 1) * SUBLANES, :], 1, 0))
                for d in range(taps_back)]
        out = cb_ref[:, cols] + cw_ref[taps_back:taps_back + 1, cols] * h
        for d in range(1, FFN_CONV):
            back = jnp.concatenate(wrap[taps_back - d:] + [h[:tm - d * SUBLANES, :]], axis=0)
            out = out + cw_ref[taps_back - d:taps_back - d + 1, cols] * back
        return out

    widths = [fc] * (D_FF // fc - 1) + FF_TAIL_CHUNKS
    starts = [sum(widths[:j]) for j in range(len(widths))]
    acc = jnp.zeros((tm, D_MODEL), _F32)
    h_gate, h_lin = up(starts[0], widths[0])
    act = None
    for j in range(len(widths)):
        if j + 1 < len(widths):
            next_gate, next_lin = up(starts[j + 1], widths[j + 1])
        if act is not None:
            acc = acc + _dot(act, wdn_ref[starts[j - 1]:starts[j], :])
        act = (_gelu_x2(conv(h_gate, starts[j])) * conv(h_lin, D_FF + starts[j])).astype(_BF16)
        h_gate, h_lin = next_gate, next_lin
    acc = acc + _dot(act, wdn_ref[starts[-1]:, :])
    y = _layer_norm(DN_ALPHA * x + acc, g_ref[...], b_ref[...])
    for n in range(lane_tiles):
        relay[n] = y[:, n * LANES:(n + 1) * LANES]
    o_ref[...] = jnp.concatenate(
        [jnp.concatenate([relay[n, pl.ds(permuted_start(g), SUBLANES, stride=SUBLANES), :]
                          for g in range(n_groups)], axis=0)
         for n in range(lane_tiles)], axis=1)


def _ffn_layer(x2d, batch, seq, wup, cw, cb, wdn, g, b):
    tm = TOKEN_BLOCK
    ns = seq // tm
    tok = pl.BlockSpec((tm, D_MODEL), lambda bi, si: (bi * ns + si, 0))
    return pl.pallas_call(
        _ffn_kernel,
        grid=(batch, ns),
        in_specs=[tok, _const_spec(wup.shape), _const_spec(cw.shape), _const_spec(cb.shape),
                  _const_spec(wdn.shape), _const_spec(g.shape), _const_spec(b.shape)],
        out_specs=tok,
        out_shape=jax.ShapeDtypeStruct(x2d.shape, _F32),
        scratch_shapes=[pltpu.VMEM(((FFN_CONV - 1) * SUBLANES, 2 * D_FF), _F32),
                        pltpu.VMEM((D_MODEL // LANES, tm, LANES), _F32)],
        compiler_params=_params(("arbitrary", "arbitrary")),
        name="conv_ffn",
    )(x2d, wup, cw, cb, wdn, g, b)


def _qkv_kernel(x_ref, w_ref, o_ref):
    res = _dot(x_ref[...].astype(_BF16), w_ref[...])
    for n in range(3 * N_HEADS):
        part = res[:, n * HEAD_DIM:(n + 1) * HEAD_DIM]
        if n >= 2 * N_HEADS:
            part = part * (HEAD_DIM ** -0.5)
        o_ref[n] = part.astype(_BF16)


def _qkv_proj(x2d, w):
    tm = TOKEN_BLOCK
    tokens = x2d.shape[0]
    return pl.pallas_call(
        _qkv_kernel,
        grid=(tokens // tm,),
        in_specs=[pl.BlockSpec((tm, D_MODEL), lambda i: (i, 0)), _const_spec(w.shape)],
        out_specs=pl.BlockSpec((3 * N_HEADS, tm, HEAD_DIM), lambda i: (0, i, 0)),
        out_shape=jax.ShapeDtypeStruct((3 * N_HEADS, tokens, HEAD_DIM), _BF16),
        compiler_params=_params(("arbitrary",)),
        name="kvq_proj",
    )(x2d, w)


def _attn_kernel(q_ref, k_ref, v_ref, x_ref, wo_ref, g_ref, b_ref, y_ref, o_acc, tail_acc):
    tq, tk = ATTN_Q_BLOCK, ATTN_K_BLOCK
    i = pl.program_id(1)
    row = lax.broadcasted_iota(jnp.int32, (tq, tk), 0)
    col = lax.broadcasted_iota(jnp.int32, (tq, tk), 1)
    later = (row > col).astype(_BF16)
    causal = col < row

    def logits(h, j):
        k = k_ref[h, pl.ds(pl.multiple_of(j * tk, tk), tk), :]
        z = lax.dot_general(q_ref[h], k, (((1,), (1,)), ((), ())), preferred_element_type=_F32)
        log_beta = jnp.minimum(z, 0.0) - jnp.log(1.0 + jnp.exp(-jnp.abs(z)))
        return log_beta, log_beta - z

    def values(h, j):
        return v_ref[h, pl.ds(pl.multiple_of(j * tk, tk), tk), :]

    def earlier(h, j, o, tail):
        log_beta, log_1m = logits(h, j)
        suffix = _dot(log_1m.astype(_BF16), later) + tail
        w = jnp.exp(log_beta + suffix)
        return (o + _dot(w.astype(_BF16), values(h, j)),
                tail + jnp.sum(log_1m, axis=1, keepdims=True))

    def stage_logits(h, j, on_diagonal):
        log_beta, log_1m = logits(h, j)
        if on_diagonal:
            log_1m = jnp.where(causal, log_1m, 0.0)
        return log_beta, log_1m.astype(_BF16), jnp.sum(log_1m, axis=1, keepdims=True)

    def stage_weights(h, j, on_diagonal, log_beta, suffix, tail):
        x = log_beta + suffix if tail is None else log_beta + suffix + tail
        w = jnp.exp(x)
        if on_diagonal:
            w = jnp.where(causal, w, 0.0)
        return _dot(w.astype(_BF16), values(h, j))

    def run_tiles(tiles):
        first, suffix, out = {}, {}, {}
        for t in range(len(tiles) + 2):
            if t < len(tiles):
                h, j, diag, _ = tiles[t]
                first[t] = stage_logits(h, j, diag)
            if 0 <= t - 1 < len(tiles):
                suffix[t - 1] = _dot(first[t - 1][1], later)
            if 0 <= t - 2 < len(tiles):
                h, j, diag, prev = tiles[t - 2]
                tail = None if prev is None else first[prev][2]
                out[t - 2] = stage_weights(h, j, diag, first[t - 2][0], suffix[t - 2], tail)
        return [out[t] for t in range(len(tiles))], [first[t][2] for t in range(len(tiles))]

    def alive(tail):
        return jnp.max(tail) >= EXP_ZERO_CUTOFF

    @pl.when(i == 0)
    def _():
        outs, _ = run_tiles([(h, i, True, None) for h in range(N_HEADS)])
        for h in range(N_HEADS):
            o_acc[h] = outs[h]

    @pl.when(i > 0)
    def _():
        tiles = []
        for h in range(N_HEADS):
            tiles += [(h, i, True, None), (h, i - 1, False, 2 * h)]
        outs, sums = run_tiles(tiles)
        worst = jnp.full((tq, 1), -jnp.inf, _F32)
        for h in range(N_HEADS):
            tail = sums[2 * h] + sums[2 * h + 1]
            o_acc[h] = outs[2 * h] + outs[2 * h + 1]
            tail_acc[h] = tail
            worst = jnp.maximum(worst, tail)

        @pl.when(jnp.logical_and(i > 1, alive(worst)))
        def _():
            def head(h, _):
                def cond(state):
                    j, _, _, live = state
                    return jnp.logical_and(j >= 0, live > 0)

                def body(state):
                    j, o, tail, _ = state
                    o, tail = earlier(h, j, o, tail)
                    return j - 1, o, tail, alive(tail).astype(jnp.int32)

                tail = tail_acc[h]
                state = (i - 2, o_acc[h], tail, alive(tail).astype(jnp.int32))
                o_acc[h] = lax.while_loop(cond, body, state)[1]
                return 0

            lax.fori_loop(0, N_HEADS, head, 0)

    o = jnp.concatenate([o_acc[h] for h in range(N_HEADS)], axis=1).astype(_BF16)
    y_ref[...] = _layer_norm(DN_ALPHA * x_ref[...] + _dot(o, wo_ref[...]), g_ref[...], b_ref[...])


def _attention_layer(kvq, x2d, batch, seq, wo, g, b):
    tq = ATTN_Q_BLOCK
    nq = seq // tq
    tok = pl.BlockSpec((tq, D_MODEL), lambda bi, i: (bi * nq + i, 0))
    return pl.pallas_call(
        _attn_kernel,
        grid=(batch, nq),
        in_specs=[pl.BlockSpec((N_HEADS, tq, HEAD_DIM), lambda bi, i: (2, bi * nq + i, 0)),
                  pl.BlockSpec((N_HEADS, seq, HEAD_DIM), lambda bi, i: (0, bi, 0)),
                  pl.BlockSpec((N_HEADS, seq, HEAD_DIM), lambda bi, i: (1, bi, 0)),
                  tok, _const_spec(wo.shape), _const_spec(g.shape), _const_spec(b.shape)],
        out_specs=tok,
        out_shape=jax.ShapeDtypeStruct(x2d.shape, _F32),
        scratch_shapes=[pltpu.VMEM((N_HEADS, tq, HEAD_DIM), _F32),
                        pltpu.VMEM((N_HEADS, tq, 1), _F32)],
        compiler_params=_params(("arbitrary", "arbitrary")),
        name="stickbreak_attn",
    )(kvq, kvq, kvq, x2d, wo, g, b)


def kernel(x, lru_w_in, lru_b_in, lru_conv_w, lru_conv_b, lru_w_gates, lru_b_gates, lru_a_param, lru_w_out, lru_b_out, kv_w, attn_w_q, attn_w_out, ffn_w_up, ffn_conv_w, ffn_conv_b, ffn_w_down, ln_g, ln_b):
    batch, seq, d = x.shape
    assert d == D_MODEL and seq % TOKEN_BLOCK == 0 and seq % ATTN_Q_BLOCK == 0
    assert lru_w_in.shape[0] == 1 and attn_w_q.shape[0] == 1 and ffn_w_up.shape[0] == DEPTH
    row = lambda v: v.reshape(1, -1)
    h = x.reshape(batch * seq, d)

    half_lin = jnp.concatenate([jnp.ones((D_FF,), _F32), jnp.full((D_FF,), 0.5, _F32)])

    def ffn(h, layer):
        return _ffn_layer(h, batch, seq, ffn_w_up[layer].astype(_BF16), ffn_conv_w[layer] * half_lin,
                          row(ffn_conv_b[layer] * half_lin), ffn_w_down[layer].astype(_BF16),
                          row(ln_g[layer, 1]), row(ln_b[layer, 1]))

    h = _lru_layer(h, batch, seq, lru_w_in[0].astype(_BF16), row(lru_b_in[0]), lru_conv_w[0],
                   row(lru_conv_b[0]), (0.5 * lru_w_gates[0]).astype(_BF16),
                   (0.5 * lru_b_gates[0]).reshape(LRU_HEADS, 1, 2 * LRU_BLOCK), row(lru_a_param[0]),
                   (0.5 * lru_w_out[0]).astype(_BF16), row(lru_b_out[0]), row(ln_g[0, 0]), row(ln_b[0, 0]))
    h = ffn(h, 0)
    kvq = _qkv_proj(h, jnp.concatenate([kv_w, attn_w_q[0]], axis=1).astype(_BF16))
    h = _attention_layer(kvq, h, batch, seq, attn_w_out[0].astype(_BF16),
                         row(ln_g[1, 0]), row(ln_b[1, 0]))
    h = ffn(h, 1)
    return h.reshape(batch, seq, d)
```

```python
import functools
import math

import jax
import jax.numpy as jnp
from jax import lax
from jax.experimental import pallas as pl
from jax.experimental.pallas import tpu as pltpu

D_MODEL = 1024
DEPTH = 2
LRU_WIDTH = D_MODEL
LRU_HEADS = 8
LRU_BLOCK = LRU_WIDTH // LRU_HEADS
LRU_CONV = 4
LRU_C = 8.0
N_HEADS = 8
HEAD_DIM = D_MODEL // N_HEADS
D_FF = 3 * D_MODEL
FFN_CONV = 3
DN_ALPHA = (2 * DEPTH) ** 0.25
LN_EPS = 1e-5

SUBLANES = 8
LANES = 128
TOKEN_BLOCK = 512
LRU_TOKEN_BLOCK = 1024
FF_CHUNK = 512
FF_TAIL_CHUNKS = [512]
ATTN_Q_BLOCK = 256
ATTN_K_BLOCK = 256
EXP_ZERO_CUTOFF = -104.0
VMEM_LIMIT = 52 * 1024 * 1024

_BF16 = jnp.bfloat16
_F32 = jnp.float32


def _dot(a, b):
    return jnp.dot(a, b, preferred_element_type=_F32)


def _gelu_x2(x):
    c = math.sqrt(2.0 / math.pi)
    t = jnp.tanh(x * (c + (c * 0.044715) * (x * x)))
    return x + x * t


def _layer_norm(y, g, b):
    mu = jnp.mean(y, axis=-1, keepdims=True)
    yc = y - mu
    var = jnp.mean(yc * yc, axis=-1, keepdims=True)
    return yc * lax.rsqrt(var + LN_EPS) * g + b


def _const_spec(shape):
    zeros = (0,) * len(shape)
    return pl.BlockSpec(shape, lambda *_: zeros, pipeline_mode=pl.Buffered(1))


def _params(sem):
    return pltpu.CompilerParams(dimension_semantics=sem, vmem_limit_bytes=VMEM_LIMIT)


def _lru_kernel(x_ref, win_ref, bin_ref, cw_ref, cb_ref, wg_ref, bg_ref, ap_ref,
                wout_ref, bout_ref, g_ref, b_ref, o_ref,
                xp_buf, x_tail, a_buf, u_buf, h_buf, h_carry):
    tm = x_ref.shape[0]
    W = LRU_WIDTH
    s = pl.program_id(1)

    @pl.when(s == 0)
    def _():
        x_tail[...] = jnp.zeros(x_tail.shape, _F32)
        h_carry[...] = jnp.zeros((SUBLANES, W), _F32)

    hm = tm // 2
    nt = 2 * LRU_BLOCK
    half_c_sp = (-0.5 * LRU_C) * jnp.logaddexp(-ap_ref[...], 0.0)
    sub = lax.broadcasted_iota(jnp.int32, (SUBLANES, W), 0)
    row = lax.broadcasted_iota(jnp.int32, (hm, LRU_BLOCK), 0)

    x_halves = [x_ref[k * hm:(k + 1) * hm, :].astype(_BF16) for k in range(2)]

    def in_proj_tile(k, t):
        return _dot(x_halves[k], win_ref[:, t * nt:(t + 1) * nt]) + bin_ref[:, t * nt:(t + 1) * nt]

    seg = hm // SUBLANES
    per_seg = seg // SUBLANES

    def permuted_start(g):
        return SUBLANES * SUBLANES * (g % per_seg) + g // per_seg

    def recurrence(k, xr_tiles, h_prev, side_jobs):
        results = []
        first_rows = lax.broadcasted_iota(jnp.int32, (SUBLANES, LRU_BLOCK), 0) < 1
        for n in range(LRU_HEADS):
            results += [job() for job in side_jobs[n]]
            cols = slice(n * LRU_BLOCK, (n + 1) * LRU_BLOCK)
            tile = xr_tiles[n // 2]
            xr = tile[:, (n % 2) * LRU_BLOCK:(n % 2 + 1) * LRU_BLOCK]
            for g in range(hm // SUBLANES):
                xp_buf[n, pl.ds(permuted_start(g), SUBLANES, stride=SUBLANES), :] = (
                    xr[g * SUBLANES:(g + 1) * SUBLANES, :])
            xp = xp_buf[n]
            prev = x_tail[n]
            keep = (LRU_CONV - 1) * SUBLANES
            x_tail[n] = xp[hm - keep:, :]
            wrap = [jnp.where(first_rows,
                              pltpu.roll(prev[d * SUBLANES:(d + 1) * SUBLANES, :], 1, 0),
                              pltpu.roll(xp[hm - keep + d * SUBLANES:hm - keep + (d + 1) * SUBLANES, :], 1, 0))
                    for d in range(LRU_CONV - 1)]
            xn = cb_ref[:, cols] + cw_ref[LRU_CONV - 1:LRU_CONV, cols] * xp
            for d in range(1, LRU_CONV):
                back = jnp.concatenate(wrap[LRU_CONV - 1 - d:] + [xp[:hm - d * SUBLANES, :]], axis=0)
                xn = xn + cw_ref[LRU_CONV - 1 - d:LRU_CONV - d, cols] * back
            t = jnp.tanh(_dot(xn.astype(_BF16), wg_ref[n]) + bg_ref[n])
            gate_i = 0.5 + 0.5 * t[:, :LRU_BLOCK]
            a = jnp.exp(half_c_sp[:, cols] + half_c_sp[:, cols] * t[:, LRU_BLOCK:])
            m = jnp.maximum(1.0 - a * a, 0.0)
            mult = jnp.where(m > 0.0, m * lax.rsqrt(m), 0.0)
            if k == 0:
                mult = jnp.where(jnp.logical_and(row == 0, s == 0), 1.0, mult)
            a_buf[k * hm:(k + 1) * hm, cols] = a
            u_buf[k * hm:(k + 1) * hm, cols] = mult * gate_i * xn

        def load(v):
            r = pl.multiple_of(k * hm + v * SUBLANES, SUBLANES)
            return a_buf[pl.ds(r, SUBLANES), :], u_buf[pl.ds(r, SUBLANES), :], r

        def pass1(v, state):
            h, p = state
            a, u, _ = load(v)
            return a * h + u, p * a

        end, prod = lax.fori_loop(0, seg, pass1, (jnp.zeros((SUBLANES, W), _F32),
                                                  jnp.ones((SUBLANES, W), _F32)))
        for d in (1, 2, 4):
            ok = sub >= d
            end = end + prod * jnp.where(ok, pltpu.roll(end, d, 0), 0.0)
            prod = prod * jnp.where(ok, pltpu.roll(prod, d, 0), 1.0)
        seg_out = end + prod * h_prev
        seg_in = jnp.where(sub < 1, h_prev, pltpu.roll(seg_out, 1, 0))

        def pass2(v, h):
            a, u, r = load(v)
            h = a * h + u
            for n in range(LRU_HEADS):
                h_buf[n, pl.ds(r, SUBLANES), :] = h[:, n * LRU_BLOCK:(n + 1) * LRU_BLOCK]
            return h

        lax.fori_loop(0, seg, pass2, seg_in)
        return jnp.broadcast_to(seg_out[SUBLANES - 1:SUBLANES, :], (SUBLANES, W)), results

    def gated(k, y_tiles):
        y_br = _gelu_x2(jnp.concatenate(y_tiles, axis=1))
        h = jnp.concatenate(
            [jnp.concatenate(
                [h_buf[n, pl.ds(k * hm + permuted_start(g), SUBLANES, stride=SUBLANES), :]
                 for g in range(hm // SUBLANES)], axis=0)
             for n in range(LRU_HEADS)], axis=1)
        return (h * y_br).astype(_BF16)

    def out_proj_tile(hy, t):
        return _dot(hy, wout_ref[:, t * nt:(t + 1) * nt]) + bout_ref[:, t * nt:(t + 1) * nt]

    def finish(k, mix_tiles):
        rows = slice(k * hm, (k + 1) * hm)
        mix = jnp.concatenate(mix_tiles, axis=1)
        o_ref[rows, :] = _layer_norm(DN_ALPHA * x_ref[rows, :] + mix, g_ref[...], b_ref[...])

    half = W // nt
    job = functools.partial
    xr0 = [in_proj_tile(0, half + t) for t in range(half)]
    jobs = ([[job(in_proj_tile, 0, t), job(in_proj_tile, 1, half + t)] for t in range(half)]
            + [[job(in_proj_tile, 1, t)] for t in range(half)])
    h_mid, res = recurrence(0, xr0, h_carry[...], jobs)
    y0, xr1, y1 = res[0:2 * half:2], res[1:2 * half:2], res[2 * half:]
    hy0 = gated(0, y0)
    n_out = D_MODEL // nt
    jobs = [[job(out_proj_tile, hy0, t)] for t in range(n_out)] + [[]] * (LRU_HEADS - n_out)
    h_last, mix0 = recurrence(1, xr1, h_mid, jobs)
    h_carry[...] = h_last
    finish(0, mix0)
    hy1 = gated(1, y1)
    finish(1, [out_proj_tile(hy1, t) for t in range(n_out)])


def _lru_layer(x2d, batch, seq, win, b_in, cw, cb, wg, bg, ap, wout, bout, g, b):
    tm = LRU_TOKEN_BLOCK
    W = LRU_WIDTH
    ns = seq // tm
    tok = pl.BlockSpec((tm, D_MODEL), lambda bi, si: (bi * ns + si, 0))
    return pl.pallas_call(
        _lru_kernel,
        grid=(batch, ns),
        in_specs=[tok, _const_spec(win.shape), _const_spec(b_in.shape), _const_spec(cw.shape),
                  _const_spec(cb.shape), _const_spec(wg.shape), _const_spec(bg.shape),
                  _const_spec(ap.shape), _const_spec(wout.shape), _const_spec(bout.shape),
                  _const_spec(g.shape), _const_spec(b.shape)],
        out_specs=tok,
        out_shape=jax.ShapeDtypeStruct(x2d.shape, _F32),
        scratch_shapes=[pltpu.VMEM((LRU_HEADS, tm // 2, LRU_BLOCK), _F32),
                        pltpu.VMEM((LRU_HEADS, (LRU_CONV - 1) * SUBLANES, LRU_BLOCK), _F32),
                        pltpu.VMEM((tm, W), _F32), pltpu.VMEM((tm, W), _F32),
                        pltpu.VMEM((LRU_HEADS, tm, LRU_BLOCK), _F32),
                        pltpu.VMEM((SUBLANES, W), _F32)],
        compiler_params=_params(("arbitrary", "arbitrary")),
        name="lru_layer",
    )(x2d, win, b_in, cw, cb, wg, bg, ap, wout, bout, g, b)


def _ffn_kernel(x_ref, wup_ref, cw_ref, cb_ref, wdn_ref, g_ref, b_ref, o_ref, carry, relay):
    tm = x_ref.shape[0]
    fc = FF_CHUNK
    s = pl.program_id(1)
    n_groups = tm // SUBLANES
    lane_tiles = D_MODEL // LANES
    taps_back = FFN_CONV - 1

    @pl.when(s == 0)
    def _():
        carry[...] = jnp.zeros(carry.shape, _F32)

    per_seg = n_groups // SUBLANES

    def permuted_start(g):
        return SUBLANES * SUBLANES * (g % per_seg) + g // per_seg

    x_nat = x_ref[...]
    for n in range(lane_tiles):
        for g in range(n_groups):
            relay[n, pl.ds(permuted_start(g), SUBLANES, stride=SUBLANES), :] = (
                x_nat[g * SUBLANES:(g + 1) * SUBLANES, n * LANES:(n + 1) * LANES])
    x = jnp.concatenate([relay[n] for n in range(lane_tiles)], axis=1)
    xb = x.astype(_BF16)

    def up(c0, width):
        return (_dot(xb, wup_ref[:, c0:c0 + width]),
                _dot(xb, wup_ref[:, D_FF + c0:D_FF + c0 + width]))

    def conv(h, col0):
        cols = slice(col0, col0 + h.shape[1])
        first_rows = lax.broadcasted_iota(jnp.int32, (SUBLANES, h.shape[1]), 0) < 1
        keep = taps_back * SUBLANES
        prev = carry[:, cols]
        carry[:, cols] = h[tm - keep:, :]
        wrap = [jnp.where(first_rows,
                          pltpu.roll(prev[d * SUBLANES:(d + 1) * SUBLANES, :], 1, 0),
                          pltpu.roll(h[tm - keep + d * SUBLANES:tm - keep + (d + 1) * SUBLANES, :], 1, 0))
                for d in range(taps_back)]
        out = cb_ref[:, cols] + cw_ref[taps_back:taps_back + 1, cols] * h
        for d in range(1, FFN_CONV):
            back = jnp.concatenate(wrap[taps_back - d:] + [h[:tm - d * SUBLANES, :]], axis=0)
            out = out + cw_ref[taps_back - d:taps_back - d + 1, cols] * back
        return out

    widths = [fc] * (D_FF // fc - 1) + FF_TAIL_CHUNKS
    starts = [sum(widths[:j]) for j in range(len(widths))]
    acc = jnp.zeros((tm, D_MODEL), _F32)
    h_gate, h_lin = up(starts[0], widths[0])
    act = None
    for j in range(len(widths)):
        if j + 1 < len(widths):
            next_gate, next_lin = up(starts[j + 1], widths[j + 1])
        if act is not None:
            acc = acc + _dot(act, wdn_ref[starts[j - 1]:starts[j], :])
        act = (_gelu_x2(conv(h_gate, starts[j])) * conv(h_lin, D_FF + starts[j])).astype(_BF16)
        h_gate, h_lin = next_gate, next_lin
    acc = acc + _dot(act, wdn_ref[starts[-1]:, :])
    y = _layer_norm(DN_ALPHA * x + acc, g_ref[...], b_ref[...])
    for n in range(lane_tiles):
        relay[n] = y[:, n * LANES:(n + 1) * LANES]
    o_ref[...] = jnp.concatenate(
        [jnp.concatenate([relay[n, pl.ds(permuted_start(g), SUBLANES, stride=SUBLANES), :]
                          for g in range(n_groups)], axis=0)
         for n in range(lane_tiles)], axis=1)


def _ffn_layer(x2d, batch, seq, wup, cw, cb, wdn, g, b):
    tm = TOKEN_BLOCK
    ns = seq // tm
    tok = pl.BlockSpec((tm, D_MODEL), lambda bi, si: (bi * ns + si, 0))
    return pl.pallas_call(
        _ffn_kernel,
        grid=(batch, ns),
        in_specs=[tok, _const_spec(wup.shape), _const_spec(cw.shape), _const_spec(cb.shape),
                  _const_spec(wdn.shape), _const_spec(g.shape), _const_spec(b.shape)],
        out_specs=tok,
        out_shape=jax.ShapeDtypeStruct(x2d.shape, _F32),
        scratch_shapes=[pltpu.VMEM(((FFN_CONV - 1) * SUBLANES, 2 * D_FF), _F32),
                        pltpu.VMEM((D_MODEL // LANES, tm, LANES), _F32)],
        compiler_params=_params(("arbitrary", "arbitrary")),
        name="conv_ffn",
    )(x2d, wup, cw, cb, wdn, g, b)


def _qkv_kernel(x_ref, w_ref, o_ref):
    res = _dot(x_ref[...].astype(_BF16), w_ref[...])
    for n in range(3 * N_HEADS):
        part = res[:, n * HEAD_DIM:(n + 1) * HEAD_DIM]
        if n >= 2 * N_HEADS:
            part = part * (HEAD_DIM ** -0.5)
        o_ref[n] = part.astype(_BF16)


def _qkv_proj(x2d, w):
    tm = TOKEN_BLOCK
    tokens = x2d.shape[0]
    return pl.pallas_call(
        _qkv_kernel,
        grid=(tokens // tm,),
        in_specs=[pl.BlockSpec((tm, D_MODEL), lambda i: (i, 0)), _const_spec(w.shape)],
        out_specs=pl.BlockSpec((3 * N_HEADS, tm, HEAD_DIM), lambda i: (0, i, 0)),
        out_shape=jax.ShapeDtypeStruct((3 * N_HEADS, tokens, HEAD_DIM), _BF16),
        compiler_params=_params(("arbitrary",)),
        name="kvq_proj",
    )(x2d, w)


def _attn_kernel(q_ref, k_ref, v_ref, x_ref, wo_ref, g_ref, b_ref, y_ref, o_acc, tail_acc, o_done,
                 *, n_blocks):
    tq, tk = ATTN_Q_BLOCK, ATTN_K_BLOCK
    i = pl.program_id(1)
    row = lax.broadcasted_iota(jnp.int32, (tq, tk), 0)
    col = lax.broadcasted_iota(jnp.int32, (tq, tk), 1)
    later = (row > col).astype(_BF16)
    causal = col < row

    def logits(h, j):
        k = k_ref[h, pl.ds(pl.multiple_of(j * tk, tk), tk), :]
        z = lax.dot_general(q_ref[h], k, (((1,), (1,)), ((), ())), preferred_element_type=_F32)
        log_beta = jnp.minimum(z, 0.0) - jnp.log(1.0 + jnp.exp(-jnp.abs(z)))
        return log_beta, log_beta - z

    def values(h, j):
        return v_ref[h, pl.ds(pl.multiple_of(j * tk, tk), tk), :]

    def earlier(h, j, o, tail):
        log_beta, log_1m = logits(h, j)
        suffix = _dot(log_1m.astype(_BF16), later) + tail
        w = jnp.exp(log_beta + suffix)
        return (o + _dot(w.astype(_BF16), values(h, j)),
                tail + jnp.sum(log_1m, axis=1, keepdims=True))

    def stage_logits(h, j, on_diagonal):
        log_beta, log_1m = logits(h, j)
        if on_diagonal:
            log_1m = jnp.where(causal, log_1m, 0.0)
        return log_beta, log_1m.astype(_BF16), jnp.sum(log_1m, axis=1, keepdims=True)

    def stage_weights(h, j, on_diagonal, log_beta, suffix, tail):
        x = log_beta + suffix if tail is None else log_beta + suffix + tail
        w = jnp.exp(x)
        if on_diagonal:
            w = jnp.where(causal, w, 0.0)
        return _dot(w.astype(_BF16), values(h, j))

    def run_tiles(tiles, side_jobs=()):
        first, suffix, out, side = {}, {}, {}, []
        n_iter = len(tiles) + 2
        for t in range(n_iter):
            side += [job() for job in
                     side_jobs[t * len(side_jobs) // n_iter:(t + 1) * len(side_jobs) // n_iter]]
            if t < len(tiles):
                h, j, diag, _ = tiles[t]
                first[t] = stage_logits(h, j, diag)
            if 0 <= t - 1 < len(tiles):
                suffix[t - 1] = _dot(first[t - 1][1], later)
            if 0 <= t - 2 < len(tiles):
                h, j, diag, prev = tiles[t - 2]
                tail = None if prev is None else first[prev][2]
                out[t - 2] = stage_weights(h, j, diag, first[t - 2][0], suffix[t - 2], tail)
        return [out[t] for t in range(len(tiles))], [first[t][2] for t in range(len(tiles))], side

    def out_proj_jobs(o):
        def tile(t):
            return _dot(o, wo_ref[:, t * tk:(t + 1) * tk])
        return [functools.partial(tile, t) for t in range(D_MODEL // tk)]

    def finish_previous(mix_tiles):
        mix = jnp.concatenate(mix_tiles, axis=1)
        y_ref[...] = _layer_norm(DN_ALPHA * x_ref[...] + mix, g_ref[...], b_ref[...])

    def park_outputs():
        o_done[...] = jnp.concatenate([o_acc[h] for h in range(N_HEADS)], axis=1).astype(_BF16)

    def alive(tail):
        return jnp.max(tail) >= EXP_ZERO_CUTOFF

    @pl.when(i == 0)
    def _():
        outs, _, _ = run_tiles([(h, i, True, None) for h in range(N_HEADS)])
        for h in range(N_HEADS):
            o_acc[h] = outs[h]
        park_outputs()

    @pl.when(jnp.logical_and(i > 0, i < n_blocks))
    def _():
        tiles = []
        for h in range(N_HEADS):
            tiles += [(h, i, True, None), (h, i - 1, False, 2 * h)]
        outs, sums, mix = run_tiles(tiles, out_proj_jobs(o_done[...]))
        finish_previous(mix)
        worst = jnp.full((tq, 1), -jnp.inf, _F32)
        for h in range(N_HEADS):
            tail = sums[2 * h] + sums[2 * h + 1]
            o_acc[h] = outs[2 * h] + outs[2 * h + 1]
            tail_acc[h] = tail
            worst = jnp.maximum(worst, tail)

        @pl.when(jnp.logical_and(i > 1, alive(worst)))
        def _():
            def head(h, _):
                def cond(state):
                    j, _, _, live = state
                    return jnp.logical_and(j >= 0, live > 0)

                def body(state):
                    j, o, tail, _ = state
                    o, tail = earlier(h, j, o, tail)
                    return j - 1, o, tail, alive(tail).astype(jnp.int32)

                tail = tail_acc[h]
                state = (i - 2, o_acc[h], tail, alive(tail).astype(jnp.int32))
                o_acc[h] = lax.while_loop(cond, body, state)[1]
                return 0

            lax.fori_loop(0, N_HEADS, head, 0)

        park_outputs()

    @pl.when(i == n_blocks)
    def _():
        finish_previous([job() for job in out_proj_jobs(o_done[...])])


def _attention_layer(kvq, x2d, batch, seq, wo, g, b):
    tq = ATTN_Q_BLOCK
    nq = seq // tq
    tok_prev = pl.BlockSpec((tq, D_MODEL), lambda bi, i: (bi * nq + jnp.maximum(i - 1, 0), 0))
    return pl.pallas_call(
        functools.partial(_attn_kernel, n_blocks=nq),
        grid=(batch, nq + 1),
        in_specs=[pl.BlockSpec((N_HEADS, tq, HEAD_DIM),
                               lambda bi, i: (2, bi * nq + jnp.minimum(i, nq - 1), 0)),
                  pl.BlockSpec((N_HEADS, seq, HEAD_DIM), lambda bi, i: (0, bi, 0)),
                  pl.BlockSpec((N_HEADS, seq, HEAD_DIM), lambda bi, i: (1, bi, 0)),
                  tok_prev, _const_spec(wo.shape), _const_spec(g.shape), _const_spec(b.shape)],
        out_specs=tok_prev,
        out_shape=jax.ShapeDtypeStruct(x2d.shape, _F32),
        scratch_shapes=[pltpu.VMEM((N_HEADS, tq, HEAD_DIM), _F32),
                        pltpu.VMEM((N_HEADS, tq, 1), _F32),
                        pltpu.VMEM((tq, N_HEADS * HEAD_DIM), _BF16)],
        compiler_params=_params(("arbitrary", "arbitrary")),
        name="stickbreak_attn",
    )(kvq, kvq, kvq, x2d, wo, g, b)


def kernel(x, lru_w_in, lru_b_in, lru_conv_w, lru_conv_b, lru_w_gates, lru_b_gates, lru_a_param, lru_w_out, lru_b_out, kv_w, attn_w_q, attn_w_out, ffn_w_up, ffn_conv_w, ffn_conv_b, ffn_w_down, ln_g, ln_b):
    batch, seq, d = x.shape
    assert d == D_MODEL and seq % TOKEN_BLOCK == 0 and seq % ATTN_Q_BLOCK == 0
    assert lru_w_in.shape[0] == 1 and attn_w_q.shape[0] == 1 and ffn_w_up.shape[0] == DEPTH
    row = lambda v: v.reshape(1, -1)
    h = x.reshape(batch * seq, d)

    half_lin = jnp.concatenate([jnp.ones((D_FF,), _F32), jnp.full((D_FF,), 0.5, _F32)])

    def ffn(h, layer):
        return _ffn_layer(h, batch, seq, ffn_w_up[layer].astype(_BF16), ffn_conv_w[layer] * half_lin,
                          row(ffn_conv_b[layer] * half_lin), ffn_w_down[layer].astype(_BF16),
                          row(ln_g[layer, 1]), row(ln_b[layer, 1]))

    h = _lru_layer(h, batch, seq, lru_w_in[0].astype(_BF16), row(lru_b_in[0]), lru_conv_w[0],
                   row(lru_conv_b[0]), (0.5 * lru_w_gates[0]).astype(_BF16),
                   (0.5 * lru_b_gates[0]).reshape(LRU_HEADS, 1, 2 * LRU_BLOCK), row(lru_a_param[0]),
                   (0.5 * lru_w_out[0]).astype(_BF16), row(lru_b_out[0]), row(ln_g[0, 0]), row(ln_b[0, 0]))
    h = ffn(h, 0)
    kvq = _qkv_proj(h, jnp.concatenate([kv_w, attn_w_q[0]], axis=1).astype(_BF16))
    h = _attention_layer(kvq, h, batch, seq, attn_w_out[0].astype(_BF16),
                         row(ln_g[1, 0]), row(ln_b[1, 0]))
    h = ffn(h, 1)
    return h.reshape(batch, seq, d)
```

```python
import functools
import math

import jax
import jax.numpy as jnp
from jax import lax
from jax.experimental import pallas as pl
from jax.experimental.pallas import tpu as pltpu

D_MODEL = 1024
DEPTH = 2
LRU_WIDTH = D_MODEL
LRU_HEADS = 8
LRU_BLOCK = LRU_WIDTH // LRU_HEADS
LRU_CONV = 4
LRU_C = 8.0
N_HEADS = 8
HEAD_DIM = D_MODEL // N_HEADS
D_FF = 3 * D_MODEL
FFN_CONV = 3
DN_ALPHA = (2 * DEPTH) ** 0.25
LN_EPS = 1e-5

SUBLANES = 8
LANES = 128
TOKEN_BLOCK = 512
LRU_TOKEN_BLOCK = 1024
FF_CHUNK = 512
FF_TAIL_CHUNKS = [512]
ATTN_Q_BLOCK = 256
ATTN_K_BLOCK = 256
EXP_ZERO_CUTOFF = -104.0
VMEM_LIMIT = 52 * 1024 * 1024

_BF16 = jnp.bfloat16
_F32 = jnp.float32


def _dot(a, b):
    return jnp.dot(a, b, preferred_element_type=_F32)


def _gelu_x2(x):
    c = math.sqrt(2.0 / math.pi)
    t = jnp.tanh(x * (c + (c * 0.044715) * (x * x)))
    return x + x * t


def _layer_norm(y, g, b):
    mu = jnp.mean(y, axis=-1, keepdims=True)
    yc = y - mu
    var = jnp.mean(yc * yc, axis=-1, keepdims=True)
    return yc * lax.rsqrt(var + LN_EPS) * g + b


def _const_spec(shape):
    zeros = (0,) * len(shape)
    return pl.BlockSpec(shape, lambda *_: zeros, pipeline_mode=pl.Buffered(1))


def _params(sem):
    return pltpu.CompilerParams(dimension_semantics=sem, vmem_limit_bytes=VMEM_LIMIT)


def _lru_kernel(x_ref, win_ref, bin_ref, cw_ref, cb_ref, wg_ref, bg_ref, ap_ref,
                wout_ref, bout_ref, g_ref, b_ref, o_ref,
                xp_buf, x_tail, a_buf, u_buf, h_buf, h_carry):
    tm = x_ref.shape[0]
    W = LRU_WIDTH
    s = pl.program_id(1)

    @pl.when(s == 0)
    def _():
        x_tail[...] = jnp.zeros(x_tail.shape, _F32)
        h_carry[...] = jnp.zeros((SUBLANES, W), _F32)

    hm = tm // 2
    nt = 2 * LRU_BLOCK
    half_c_sp = (-0.5 * LRU_C) * jnp.logaddexp(-ap_ref[...], 0.0)
    sub = lax.broadcasted_iota(jnp.int32, (SUBLANES, W), 0)
    row = lax.broadcasted_iota(jnp.int32, (hm, LRU_BLOCK), 0)

    x_halves = [x_ref[k * hm:(k + 1) * hm, :].astype(_BF16) for k in range(2)]

    def in_proj_tile(k, t):
        return _dot(x_halves[k], win_ref[:, t * nt:(t + 1) * nt]) + bin_ref[:, t * nt:(t + 1) * nt]

    seg = hm // SUBLANES
    per_seg = seg // SUBLANES

    def permuted_start(g):
        return SUBLANES * SUBLANES * (g % per_seg) + g // per_seg

    def recurrence(k, xr_tiles, h_prev, side_jobs):
        results = []
        first_rows = lax.broadcasted_iota(jnp.int32, (SUBLANES, LRU_BLOCK), 0) < 1
        for n in range(LRU_HEADS):
            results += [job() for job in side_jobs[n]]
            cols = slice(n * LRU_BLOCK, (n + 1) * LRU_BLOCK)
            tile = xr_tiles[n // 2]
            xr = tile[:, (n % 2) * LRU_BLOCK:(n % 2 + 1) * LRU_BLOCK]
            for g in range(hm // SUBLANES):
                xp_buf[n, pl.ds(permuted_start(g), SUBLANES, stride=SUBLANES), :] = (
                    xr[g * SUBLANES:(g + 1) * SUBLANES, :])
            xp = xp_buf[n]
            prev = x_tail[n]
            keep = (LRU_CONV - 1) * SUBLANES
            x_tail[n] = xp[hm - keep:, :]
            wrap = [jnp.where(first_rows,
                              pltpu.roll(prev[d * SUBLANES:(d + 1) * SUBLANES, :], 1, 0),
                              pltpu.roll(xp[hm - keep + d * SUBLANES:hm - keep + (d + 1) * SUBLANES, :], 1, 0))
                    for d in range(LRU_CONV - 1)]
            xn = cb_ref[:, cols] + cw_ref[LRU_CONV - 1:LRU_CONV, cols] * xp
            for d in range(1, LRU_CONV):
                back = jnp.concatenate(wrap[LRU_CONV - 1 - d:] + [xp[:hm - d * SUBLANES, :]], axis=0)
                xn = xn + cw_ref[LRU_CONV - 1 - d:LRU_CONV - d, cols] * back
            t = jnp.tanh(_dot(xn.astype(_BF16), wg_ref[n]) + bg_ref[n])
            gate_i = 0.5 + 0.5 * t[:, :LRU_BLOCK]
            a = jnp.exp(half_c_sp[:, cols] + half_c_sp[:, cols] * t[:, LRU_BLOCK:])
            m = jnp.maximum(1.0 - a * a, 0.0)
            mult = jnp.where(m > 0.0, m * lax.rsqrt(m), 0.0)
            if k == 0:
                mult = jnp.where(jnp.logical_and(row == 0, s == 0), 1.0, mult)
            a_buf[k * hm:(k + 1) * hm, cols] = a
            u_buf[k * hm:(k + 1) * hm, cols] = mult * gate_i * xn

        def load(v):
            r = pl.multiple_of(k * hm + v * SUBLANES, SUBLANES)
            return a_buf[pl.ds(r, SUBLANES), :], u_buf[pl.ds(r, SUBLANES), :], r

        def pass1(v, state):
            h, p = state
            a, u, _ = load(v)
            return a * h + u, p * a

        end, prod = lax.fori_loop(0, seg, pass1, (jnp.zeros((SUBLANES, W), _F32),
                                                  jnp.ones((SUBLANES, W), _F32)))
        for d in (1, 2, 4):
            ok = sub >= d
            end = end + prod * jnp.where(ok, pltpu.roll(end, d, 0), 0.0)
            prod = prod * jnp.where(ok, pltpu.roll(prod, d, 0), 1.0)
        seg_out = end + prod * h_prev
        seg_in = jnp.where(sub < 1, h_prev, pltpu.roll(seg_out, 1, 0))

        def pass2(v, h):
            a, u, r = load(v)
            h = a * h + u
            for n in range(LRU_HEADS):
                h_buf[n, pl.ds(r, SUBLANES), :] = h[:, n * LRU_BLOCK:(n + 1) * LRU_BLOCK]
            return h

        lax.fori_loop(0, seg, pass2, seg_in)
        return jnp.broadcast_to(seg_out[SUBLANES - 1:SUBLANES, :], (SUBLANES, W)), results

    def gated(k, y_tiles):
        y_br = _gelu_x2(jnp.concatenate(y_tiles, axis=1))
        h = jnp.concatenate(
            [jnp.concatenate(
                [h_buf[n, pl.ds(k * hm + permuted_start(g), SUBLANES, stride=SUBLANES), :]
                 for g in range(hm // SUBLANES)], axis=0)
             for n in range(LRU_HEADS)], axis=1)
        return (h * y_br).astype(_BF16)

    def out_proj_tile(hy, t):
        return _dot(hy, wout_ref[:, t * nt:(t + 1) * nt]) + bout_ref[:, t * nt:(t + 1) * nt]

    def finish(k, mix_tiles):
        rows = slice(k * hm, (k + 1) * hm)
        mix = jnp.concatenate(mix_tiles, axis=1)
        o_ref[rows, :] = _layer_norm(DN_ALPHA * x_ref[rows, :] + mix, g_ref[...], b_ref[...])

    half = W // nt
    job = functools.partial
    xr0 = [in_proj_tile(0, half + t) for t in range(half)]
    jobs = ([[job(in_proj_tile, 0, t), job(in_proj_tile, 1, half + t)] for t in range(half)]
            + [[job(in_proj_tile, 1, t)] for t in range(half)])
    h_mid, res = recurrence(0, xr0, h_carry[...], jobs)
    y0, xr1, y1 = res[0:2 * half:2], res[1:2 * half:2], res[2 * half:]
    hy0 = gated(0, y0)
    n_out = D_MODEL // nt
    jobs = [[job(out_proj_tile, hy0, t)] for t in range(n_out)] + [[]] * (LRU_HEADS - n_out)
    h_last, mix0 = recurrence(1, xr1, h_mid, jobs)
    h_carry[...] = h_last
    finish(0, mix0)
    hy1 = gated(1, y1)
    finish(1, [out_proj_tile(hy1, t) for t in range(n_out)])


def _lru_layer(x2d, batch, seq, win, b_in, cw, cb, wg, bg, ap, wout, bout, g, b):
    tm = LRU_TOKEN_BLOCK
    W = LRU_WIDTH
    ns = seq // tm
    tok = pl.BlockSpec((tm, D_MODEL), lambda bi, si: (bi * ns + si, 0))
    return pl.pallas_call(
        _lru_kernel,
        grid=(batch, ns),
        in_specs=[tok, _const_spec(win.shape), _const_spec(b_in.shape), _const_spec(cw.shape),
                  _const_spec(cb.shape), _const_spec(wg.shape), _const_spec(bg.shape),
                  _const_spec(ap.shape), _const_spec(wout.shape), _const_spec(bout.shape),
                  _const_spec(g.shape), _const_spec(b.shape)],
        out_specs=tok,
        out_shape=jax.ShapeDtypeStruct(x2d.shape, _F32),
        scratch_shapes=[pltpu.VMEM((LRU_HEADS, tm // 2, LRU_BLOCK), _F32),
                        pltpu.VMEM((LRU_HEADS, (LRU_CONV - 1) * SUBLANES, LRU_BLOCK), _F32),
                        pltpu.VMEM((tm, W), _F32), pltpu.VMEM((tm, W), _F32),
                        pltpu.VMEM((LRU_HEADS, tm, LRU_BLOCK), _F32),
                        pltpu.VMEM((SUBLANES, W), _F32)],
        compiler_params=_params(("arbitrary", "arbitrary")),
        name="lru_layer",
    )(x2d, win, b_in, cw, cb, wg, bg, ap, wout, bout, g, b)


def _ffn_kernel(x_ref, wup_ref, cw_ref, cb_ref, wdn_ref, g_ref, b_ref, o_ref, carry, relay):
    tm = x_ref.shape[0]
    fc = FF_CHUNK
    s = pl.program_id(1)
    n_groups = tm // SUBLANES
    lane_tiles = D_MODEL // LANES
    taps_back = FFN_CONV - 1

    @pl.when(s == 0)
    def _():
        carry[...] = jnp.zeros(carry.shape, _F32)

    per_seg = n_groups // SUBLANES

    def permuted_start(g):
        return SUBLANES * SUBLANES * (g % per_seg) + g // per_seg

    x_nat = x_ref[...]
    for n in range(lane_tiles):
        for g in range(n_groups):
            relay[n, pl.ds(permuted_start(g), SUBLANES, stride=SUBLANES), :] = (
                x_nat[g * SUBLANES:(g + 1) * SUBLANES, n * LANES:(n + 1) * LANES])
    x = jnp.concatenate([relay[n] for n in range(lane_tiles)], axis=1)
    xb = x.astype(_BF16)

    def up(c0, width):
        return (_dot(xb, wup_ref[:, c0:c0 + width]),
                _dot(xb, wup_ref[:, D_FF + c0:D_FF + c0 + width]))

    def conv(h, col0):
        cols = slice(col0, col0 + h.shape[1])
        first_rows = lax.broadcasted_iota(jnp.int32, (SUBLANES, h.shape[1]), 0) < 1
        keep = taps_back * SUBLANES
        prev = carry[:, cols]
        carry[:, cols] = h[tm - keep:, :]
        wrap = [jnp.where(first_rows,
                          pltpu.roll(prev[d * SUBLANES:(d + 1) * SUBLANES, :], 1, 0),
                          pltpu.roll(h[tm - keep + d * SUBLANES:tm - keep + (d + 1) * SUBLANES, :], 1, 0))
                for d in range(taps_back)]
        out = cb_ref[:, cols] + cw_ref[taps_back:taps_back + 1, cols] * h
        for d in range(1, FFN_CONV):
            back = jnp.concatenate(wrap[taps_back - d:] + [h[:tm - d * SUBLANES, :]], axis=0)
            out = out + cw_ref[taps_back - d:taps_back - d + 1, cols] * back
        return out

    widths = [fc] * (D_FF // fc - 1) + FF_TAIL_CHUNKS
    starts = [sum(widths[:j]) for j in range(len(widths))]
    acc = jnp.zeros((tm, D_MODEL), _F32)
    h_gate, h_lin = up(starts[0], widths[0])
    act = None
    for j in range(len(widths)):
        if j + 1 < len(widths):
            next_gate, next_lin = up(starts[j + 1], widths[j + 1])
        if act is not None:
            acc = acc + _dot(act, wdn_ref[starts[j - 1]:starts[j], :])
        act = (_gelu_x2(conv(h_gate, starts[j])) * conv(h_lin, D_FF + starts[j])).astype(_BF16)
        h_gate, h_lin = next_gate, next_lin
    acc = acc + _dot(act, wdn_ref[starts[-1]:, :])
    y = _layer_norm(DN_ALPHA * x + acc, g_ref[...], b_ref[...])
    for n in range(lane_tiles):
        relay[n] = y[:, n * LANES:(n + 1) * LANES]
    o_ref[...] = jnp.concatenate(
        [jnp.concatenate([relay[n, pl.ds(permuted_start(g), SUBLANES, stride=SUBLANES), :]
                          for g in range(n_groups)], axis=0)
         for n in range(lane_tiles)], axis=1)


def _ffn_layer(x2d, batch, seq, wup, cw, cb, wdn, g, b):
    tm = TOKEN_BLOCK
    ns = seq // tm
    tok = pl.BlockSpec((tm, D_MODEL), lambda bi, si: (bi * ns + si, 0))
    return pl.pallas_call(
        _ffn_kernel,
        grid=(batch, ns),
        in_specs=[tok, _const_spec(wup.shape), _const_spec(cw.shape), _const_spec(cb.shape),
                  _const_spec(wdn.shape), _const_spec(g.shape), _const_spec(b.shape)],
        out_specs=tok,
        out_shape=jax.ShapeDtypeStruct(x2d.shape, _F32),
        scratch_shapes=[pltpu.VMEM(((FFN_CONV - 1) * SUBLANES, 2 * D_FF), _F32),
                        pltpu.VMEM((D_MODEL // LANES, tm, LANES), _F32)],
        compiler_params=_params(("arbitrary", "arbitrary")),
        name="conv_ffn",
    )(x2d, wup, cw, cb, wdn, g, b)


def _qkv_kernel(x_ref, w_ref, o_ref):
    res = _dot(x_ref[...].astype(_BF16), w_ref[...])
    for n in range(3 * N_HEADS):
        part = res[:, n * HEAD_DIM:(n + 1) * HEAD_DIM]
        if n >= 2 * N_HEADS:
            part = part * (HEAD_DIM ** -0.5)
        o_ref[n] = part.astype(_BF16)


def _qkv_proj(x2d, w):
    tm = TOKEN_BLOCK
    tokens = x2d.shape[0]
    return pl.pallas_call(
        _qkv_kernel,
        grid=(tokens // tm,),
        in_specs=[pl.BlockSpec((tm, D_MODEL), lambda i: (i, 0)), _const_spec(w.shape)],
        out_specs=pl.BlockSpec((3 * N_HEADS, tm, HEAD_DIM), lambda i: (0, i, 0)),
        out_shape=jax.ShapeDtypeStruct((3 * N_HEADS, tokens, HEAD_DIM), _BF16),
        compiler_params=_params(("arbitrary",)),
        name="kvq_proj",
    )(x2d, w)


def _attn_kernel(q_ref, k_ref, v_ref, x_ref, wo_ref, g_ref, b_ref, y_ref, o_acc, tail_acc, o_done,
                 *, n_blocks):
    tq, tk = ATTN_Q_BLOCK, ATTN_K_BLOCK
    i = pl.program_id(1)
    row = lax.broadcasted_iota(jnp.int32, (tq, tk), 0)
    col = lax.broadcasted_iota(jnp.int32, (tq, tk), 1)
    later = (row > col).astype(_BF16)
    causal = col < row

    def logits(h, j):
        k = k_ref[h, pl.ds(pl.multiple_of(j * tk, tk), tk), :]
        z = lax.dot_general(q_ref[h], k, (((1,), (1,)), ((), ())), preferred_element_type=_F32)
        log_beta = jnp.minimum(z, 0.0) - jnp.log(1.0 + jnp.exp(-jnp.abs(z)))
        return log_beta, log_beta - z

    def values(h, j):
        return v_ref[h, pl.ds(pl.multiple_of(j * tk, tk), tk), :]

    def earlier(h, j, o, tail):
        log_beta, log_1m = logits(h, j)
        suffix = _dot(log_1m.astype(_BF16), later) + tail
        w = jnp.exp(log_beta + suffix)
        return (o + _dot(w.astype(_BF16), values(h, j)),
                tail + jnp.sum(log_1m, axis=1, keepdims=True))

    def stage_logits(h, j, on_diagonal):
        log_beta, log_1m = logits(h, j)
        if on_diagonal:
            log_1m = jnp.where(causal, log_1m, 0.0)
        return log_beta, log_1m.astype(_BF16), jnp.sum(log_1m, axis=1, keepdims=True)

    def stage_weights(h, j, on_diagonal, log_beta, suffix, tail):
        x = log_beta + suffix if tail is None else log_beta + suffix + tail
        w = jnp.exp(x)
        if on_diagonal:
            w = jnp.where(causal, w, 0.0)
        return _dot(w.astype(_BF16), values(h, j))

    def run_tiles(tiles, side_jobs=()):
        first, suffix, out, side = {}, {}, {}, []
        n_iter = len(tiles) + 2
        for t in range(n_iter):
            side += [job() for job in
                     side_jobs[t * len(side_jobs) // n_iter:(t + 1) * len(side_jobs) // n_iter]]
            if t < len(tiles):
                h, j, diag, _ = tiles[t]
                first[t] = stage_logits(h, j, diag)
            if 0 <= t - 1 < len(tiles):
                suffix[t - 1] = _dot(first[t - 1][1], later)
            if 0 <= t - 2 < len(tiles):
                h, j, diag, prev = tiles[t - 2]
                tail = None if prev is None else first[prev][2]
                out[t - 2] = stage_weights(h, j, diag, first[t - 2][0], suffix[t - 2], tail)
        return [out[t] for t in range(len(tiles))], [first[t][2] for t in range(len(tiles))], side

    def out_proj_jobs(o):
        def tile(t):
            return _dot(o, wo_ref[:, t * tk:(t + 1) * tk])
        return [functools.partial(tile, t) for t in range(D_MODEL // tk)]

    def finish_previous(mix_tiles):
        mix = jnp.concatenate(mix_tiles, axis=1)
        y_ref[...] = _layer_norm(DN_ALPHA * x_ref[...] + mix, g_ref[...], b_ref[...])

    def park_outputs():
        o_done[...] = jnp.concatenate([o_acc[h] for h in range(N_HEADS)], axis=1).astype(_BF16)

    def alive(tail):
        return jnp.max(tail) >= EXP_ZERO_CUTOFF

    @pl.when(i == 0)
    def _():
        outs, _, _ = run_tiles([(h, i, True, None) for h in range(N_HEADS)])
        for h in range(N_HEADS):
            o_acc[h] = outs[h]
        park_outputs()

    @pl.when(jnp.logical_and(i > 0, i < n_blocks))
    def _():
        tiles = []
        for h in range(N_HEADS):
            tiles += [(h, i, True, None), (h, i - 1, False, 2 * h)]
        outs, sums, mix = run_tiles(tiles, out_proj_jobs(o_done[...]))
        finish_previous(mix)
        worst = jnp.full((tq, 1), -jnp.inf, _F32)
        for h in range(N_HEADS):
            tail = sums[2 * h] + sums[2 * h + 1]
            o_acc[h] = outs[2 * h] + outs[2 * h + 1]
            tail_acc[h] = tail
            worst = jnp.maximum(worst, tail)

        @pl.when(jnp.logical_and(i > 1, alive(worst)))
        def _():
            def head(h, _):
                def cond(state):
                    j, _, _, live = state
                    return jnp.logical_and(j >= 0, live > 0)

                def body(state):
                    j, o, tail, _ = state
                    o, tail = earlier(h, j, o, tail)
                    return j - 1, o, tail, alive(tail).astype(jnp.int32)

                tail = tail_acc[h]
                state = (i - 2, o_acc[h], tail, alive(tail).astype(jnp.int32))
                o_acc[h] = lax.while_loop(cond, body, state)[1]
                return 0

            lax.fori_loop(0, N_HEADS, head, 0)

        park_outputs()

    @pl.when(i == n_blocks)
    def _():
        finish_previous([job() for job in out_proj_jobs(o_done[...])])


def _attention_layer(kvq, x2d, batch, seq, wo, g, b):
    tq = ATTN_Q_BLOCK
    nq = seq // tq
    tok_prev = pl.BlockSpec((tq, D_MODEL), lambda bi, i: (bi * nq + jnp.maximum(i - 1, 0), 0))

    def kv_batch(bi, i):
        return jnp.minimum(bi + i // nq, batch - 1)

    return pl.pallas_call(
        functools.partial(_attn_kernel, n_blocks=nq),
        grid=(batch, nq + 1),
        in_specs=[pl.BlockSpec((N_HEADS, tq, HEAD_DIM),
                               lambda bi, i: (2, bi * nq + jnp.minimum(i, nq - 1), 0)),
                  pl.BlockSpec((N_HEADS, seq, HEAD_DIM), lambda bi, i: (0, kv_batch(bi, i), 0)),
                  pl.BlockSpec((N_HEADS, seq, HEAD_DIM), lambda bi, i: (1, kv_batch(bi, i), 0)),
                  tok_prev, _const_spec(wo.shape), _const_spec(g.shape), _const_spec(b.shape)],
        out_specs=tok_prev,
        out_shape=jax.ShapeDtypeStruct(x2d.shape, _F32),
        scratch_shapes=[pltpu.VMEM((N_HEADS, tq, HEAD_DIM), _F32),
                        pltpu.VMEM((N_HEADS, tq, 1), _F32),
                        pltpu.VMEM((tq, N_HEADS * HEAD_DIM), _BF16)],
        compiler_params=_params(("arbitrary", "arbitrary")),
        name="stickbreak_attn",
    )(kvq, kvq, kvq, x2d, wo, g, b)


def kernel(x, lru_w_in, lru_b_in, lru_conv_w, lru_conv_b, lru_w_gates, lru_b_gates, lru_a_param, lru_w_out, lru_b_out, kv_w, attn_w_q, attn_w_out, ffn_w_up, ffn_conv_w, ffn_conv_b, ffn_w_down, ln_g, ln_b):
    batch, seq, d = x.shape
    assert d == D_MODEL and seq % TOKEN_BLOCK == 0 and seq % ATTN_Q_BLOCK == 0
    assert lru_w_in.shape[0] == 1 and attn_w_q.shape[0] == 1 and ffn_w_up.shape[0] == DEPTH
    row = lambda v: v.reshape(1, -1)
    h = x.reshape(batch * seq, d)

    half_lin = jnp.concatenate([jnp.ones((D_FF,), _F32), jnp.full((D_FF,), 0.5, _F32)])

    def ffn(h, layer):
        return _ffn_layer(h, batch, seq, ffn_w_up[layer].astype(_BF16), ffn_conv_w[layer] * half_lin,
                          row(ffn_conv_b[layer] * half_lin), ffn_w_down[layer].astype(_BF16),
                          row(ln_g[layer, 1]), row(ln_b[layer, 1]))

    h = _lru_layer(h, batch, seq, lru_w_in[0].astype(_BF16), row(lru_b_in[0]), lru_conv_w[0],
                   row(lru_conv_b[0]), (0.5 * lru_w_gates[0]).astype(_BF16),
                   (0.5 * lru_b_gates[0]).reshape(LRU_HEADS, 1, 2 * LRU_BLOCK), row(lru_a_param[0]),
                   (0.5 * lru_w_out[0]).astype(_BF16), row(lru_b_out[0]), row(ln_g[0, 0]), row(ln_b[0, 0]))
    h = ffn(h, 0)
    kvq = _qkv_proj(h, jnp.concatenate([kv_w, attn_w_q[0]], axis=1).astype(_BF16))
    h = _attention_layer(kvq, h, batch, seq, attn_w_out[0].astype(_BF16),
                         row(ln_g[1, 0]), row(ln_b[1, 0]))
    h = ffn(h, 1)
    return h.reshape(batch, seq, d)
```

```python
import functools
import math

import jax
import jax.numpy as jnp
from jax import lax
from jax.experimental import pallas as pl
from jax.experimental.pallas import tpu as pltpu

D_MODEL = 1024
DEPTH = 2
LRU_WIDTH = D_MODEL
LRU_HEADS = 8
LRU_BLOCK = LRU_WIDTH // LRU_HEADS
LRU_CONV = 4
LRU_C = 8.0
N_HEADS = 8
HEAD_DIM = D_MODEL // N_HEADS
D_FF = 3 * D_MODEL
FFN_CONV = 3
DN_ALPHA = (2 * DEPTH) ** 0.25
LN_EPS = 1e-5

SUBLANES = 8
LANES = 128
TOKEN_BLOCK = 512
LRU_TOKEN_BLOCK = 1024
FF_CHUNK = 512
ATTN_Q_BLOCK = 256
ATTN_K_BLOCK = 256
EXP_ZERO_CUTOFF = -104.0
VMEM_LIMIT = 52 * 1024 * 1024

_BF16 = jnp.bfloat16
_F32 = jnp.float32


def _dot(a, b):
    return jnp.dot(a, b, preferred_element_type=_F32)


def _gelu_x2(x):
    c = math.sqrt(2.0 / math.pi)
    t = jnp.tanh(x * (c + (c * 0.044715) * (x * x)))
    return x + x * t


def _layer_norm(y, g, b):
    mu = jnp.mean(y, axis=-1, keepdims=True)
    yc = y - mu
    var = jnp.mean(yc * yc, axis=-1, keepdims=True)
    return yc * lax.rsqrt(var + LN_EPS) * g + b


def _const_spec(shape):
    zeros = (0,) * len(shape)
    return pl.BlockSpec(shape, lambda *_: zeros, pipeline_mode=pl.Buffered(1))


def _params(sem):
    return pltpu.CompilerParams(dimension_semantics=sem, vmem_limit_bytes=VMEM_LIMIT)


def _lru_kernel(x_ref, win_ref, bin_ref, cw_ref, cb_ref, wg_ref, bg_ref, ap_ref,
                wout_ref, bout_ref, g_ref, b_ref, o_ref,
                xp_buf, x_tail, a_buf, u_buf, h_buf, h_carry):
    tm = x_ref.shape[0]
    W = LRU_WIDTH
    s = pl.program_id(1)

    @pl.when(s == 0)
    def _():
        x_tail[...] = jnp.zeros(x_tail.shape, _F32)
        h_carry[...] = jnp.zeros((SUBLANES, W), _F32)

    hm = tm // 2
    nt = 2 * LRU_BLOCK
    half_c_sp = (-0.5 * LRU_C) * jnp.logaddexp(-ap_ref[...], 0.0)
    sub = lax.broadcasted_iota(jnp.int32, (SUBLANES, W), 0)
    row = lax.broadcasted_iota(jnp.int32, (hm, LRU_BLOCK), 0)

    x_halves = [x_ref[k * hm:(k + 1) * hm, :].astype(_BF16) for k in range(2)]

    def in_proj_tile(k, t):
        return _dot(x_halves[k], win_ref[:, t * nt:(t + 1) * nt]) + bin_ref[:, t * nt:(t + 1) * nt]

    seg = hm // SUBLANES
    per_seg = seg // SUBLANES

    def permuted_start(g):
        return SUBLANES * SUBLANES * (g % per_seg) + g // per_seg

    def recurrence(k, xr_tiles, h_prev, side_jobs):
        results = []
        first_rows = lax.broadcasted_iota(jnp.int32, (SUBLANES, LRU_BLOCK), 0) < 1
        for n in range(LRU_HEADS):
            results += [job() for job in side_jobs[n]]
            cols = slice(n * LRU_BLOCK, (n + 1) * LRU_BLOCK)
            tile = xr_tiles[n // 2]
            xr = tile[:, (n % 2) * LRU_BLOCK:(n % 2 + 1) * LRU_BLOCK]
            for g in range(hm // SUBLANES):
                xp_buf[n, pl.ds(permuted_start(g), SUBLANES, stride=SUBLANES), :] = (
                    xr[g * SUBLANES:(g + 1) * SUBLANES, :])
            xp = xp_buf[n]
            prev = x_tail[n]
            keep = (LRU_CONV - 1) * SUBLANES
            x_tail[n] = xp[hm - keep:, :]
            wrap = [jnp.where(first_rows,
                              pltpu.roll(prev[d * SUBLANES:(d + 1) * SUBLANES, :], 1, 0),
                              pltpu.roll(xp[hm - keep + d * SUBLANES:hm - keep + (d + 1) * SUBLANES, :], 1, 0))
                    for d in range(LRU_CONV - 1)]
            xn = cb_ref[:, cols] + cw_ref[LRU_CONV - 1:LRU_CONV, cols] * xp
            for d in range(1, LRU_CONV):
                back = jnp.concatenate(wrap[LRU_CONV - 1 - d:] + [xp[:hm - d * SUBLANES, :]], axis=0)
                xn = xn + cw_ref[LRU_CONV - 1 - d:LRU_CONV - d, cols] * back
            t = jnp.tanh(_dot(xn.astype(_BF16), wg_ref[n]) + bg_ref[n])
            gate_i = 0.5 + 0.5 * t[:, :LRU_BLOCK]
            a = jnp.exp(half_c_sp[:, cols] + half_c_sp[:, cols] * t[:, LRU_BLOCK:])
            m = jnp.maximum(1.0 - a * a, 0.0)
            mult = jnp.where(m > 0.0, m * lax.rsqrt(m), 0.0)
            if k == 0:
                mult = jnp.where(jnp.logical_and(row == 0, s == 0), 1.0, mult)
            a_buf[k * hm:(k + 1) * hm, cols] = a
            u_buf[k * hm:(k + 1) * hm, cols] = mult * gate_i * xn

        def load(v):
            r = pl.multiple_of(k * hm + v * SUBLANES, SUBLANES)
            return a_buf[pl.ds(r, SUBLANES), :], u_buf[pl.ds(r, SUBLANES), :], r

        def pass1(v, state):
            h, p = state
            a, u, _ = load(v)
            return a * h + u, p * a

        end, prod = lax.fori_loop(0, seg, pass1, (jnp.zeros((SUBLANES, W), _F32),
                                                  jnp.ones((SUBLANES, W), _F32)))
        for d in (1, 2, 4):
            ok = sub >= d
            end = end + prod * jnp.where(ok, pltpu.roll(end, d, 0), 0.0)
            prod = prod * jnp.where(ok, pltpu.roll(prod, d, 0), 1.0)
        seg_out = end + prod * h_prev
        seg_in = jnp.where(sub < 1, h_prev, pltpu.roll(seg_out, 1, 0))

        def pass2(v, h):
            a, u, r = load(v)
            h = a * h + u
            for n in range(LRU_HEADS):
                h_buf[n, pl.ds(r, SUBLANES), :] = h[:, n * LRU_BLOCK:(n + 1) * LRU_BLOCK]
            return h

        lax.fori_loop(0, seg, pass2, seg_in)
        return jnp.broadcast_to(seg_out[SUBLANES - 1:SUBLANES, :], (SUBLANES, W)), results

    def gated(k, y_tiles):
        y_br = _gelu_x2(jnp.concatenate(y_tiles, axis=1))
        h = jnp.concatenate(
            [jnp.concatenate(
                [h_buf[n, pl.ds(k * hm + permuted_start(g), SUBLANES, stride=SUBLANES), :]
                 for g in range(hm // SUBLANES)], axis=0)
             for n in range(LRU_HEADS)], axis=1)
        return (h * y_br).astype(_BF16)

    def out_proj_tile(hy, t):
        return _dot(hy, wout_ref[:, t * nt:(t + 1) * nt]) + bout_ref[:, t * nt:(t + 1) * nt]

    def finish(k, mix_tiles):
        rows = slice(k * hm, (k + 1) * hm)
        mix = jnp.concatenate(mix_tiles, axis=1)
        o_ref[rows, :] = _layer_norm(DN_ALPHA * x_ref[rows, :] + mix, g_ref[...], b_ref[...])

    half = W // nt
    job = functools.partial
    xr0 = [in_proj_tile(0, half + t) for t in range(half)]
    jobs = ([[job(in_proj_tile, 0, t), job(in_proj_tile, 1, half + t)] for t in range(half)]
            + [[job(in_proj_tile, 1, t)] for t in range(half)])
    h_mid, res = recurrence(0, xr0, h_carry[...], jobs)
    y0, xr1, y1 = res[0:2 * half:2], res[1:2 * half:2], res[2 * half:]
    hy0 = gated(0, y0)
    n_out = D_MODEL // nt
    jobs = [[job(out_proj_tile, hy0, t)] for t in range(n_out)] + [[]] * (LRU_HEADS - n_out)
    h_last, mix0 = recurrence(1, xr1, h_mid, jobs)
    h_carry[...] = h_last
    finish(0, mix0)
    hy1 = gated(1, y1)
    finish(1, [out_proj_tile(hy1, t) for t in range(n_out)])


def _lru_layer(x2d, batch, seq, win, b_in, cw, cb, wg, bg, ap, wout, bout, g, b):
    tm = LRU_TOKEN_BLOCK
    W = LRU_WIDTH
    ns = seq // tm
    tok = pl.BlockSpec((tm, D_MODEL), lambda bi, si: (bi * ns + si, 0))
    return pl.pallas_call(
        _lru_kernel,
        grid=(batch, ns),
        in_specs=[tok, _const_spec(win.shape), _const_spec(b_in.shape), _const_spec(cw.shape),
                  _const_spec(cb.shape), _const_spec(wg.shape), _const_spec(bg.shape),
                  _const_spec(ap.shape), _const_spec(wout.shape), _const_spec(bout.shape),
                  _const_spec(g.shape), _const_spec(b.shape)],
        out_specs=tok,
        out_shape=jax.ShapeDtypeStruct(x2d.shape, _F32),
        scratch_shapes=[pltpu.VMEM((LRU_HEADS, tm // 2, LRU_BLOCK), _F32),
                        pltpu.VMEM((LRU_HEADS, (LRU_CONV - 1) * SUBLANES, LRU_BLOCK), _F32),
                        pltpu.VMEM((tm, W), _F32), pltpu.VMEM((tm, W), _F32),
                        pltpu.VMEM((LRU_HEADS, tm, LRU_BLOCK), _F32),
                        pltpu.VMEM((SUBLANES, W), _F32)],
        compiler_params=_params(("arbitrary", "arbitrary")),
        name="lru_layer",
    )(x2d, win, b_in, cw, cb, wg, bg, ap, wout, bout, g, b)


def _ffn_kernel(x_ref, wup_ref, wdn_ref, cw_ref, cb_ref, g_ref, b_ref, o_ref,
                wup_bf, wdn_bf, carry, relay, *, n_cast, blocks_per_seq):
    tm = x_ref.shape[0]
    fc = FF_CHUNK
    s = pl.program_id(0)
    n_groups = tm // SUBLANES
    lane_tiles = D_MODEL // LANES
    taps_back = FFN_CONV - 1

    @pl.when(s < n_cast)
    def _():
        wup_bf[s] = wup_ref[...].astype(_BF16)

    @pl.when(s < D_FF // fc)
    def _():
        wdn_bf[s] = wdn_ref[...].astype(_BF16)

    @pl.when(jnp.logical_and(s >= n_cast, (s - n_cast) % blocks_per_seq == 0))
    def _():
        carry[...] = jnp.zeros(carry.shape, _F32)

    @pl.when(s >= n_cast)
    def _():
        per_seg = n_groups // SUBLANES

        def permuted_start(g):
            return SUBLANES * SUBLANES * (g % per_seg) + g // per_seg

        x_nat = x_ref[...]
        for n in range(lane_tiles):
            for g in range(n_groups):
                relay[n, pl.ds(permuted_start(g), SUBLANES, stride=SUBLANES), :] = (
                    x_nat[g * SUBLANES:(g + 1) * SUBLANES, n * LANES:(n + 1) * LANES])
        x = jnp.concatenate([relay[n] for n in range(lane_tiles)], axis=1)
        xb = x.astype(_BF16)

        def up(c0, width):
            return (_dot(xb, wup_bf[c0 // fc]),
                    _dot(xb, wup_bf[(D_FF + c0) // fc]))

        def conv(h, col0):
            cols = slice(col0, col0 + h.shape[1])
            first_rows = lax.broadcasted_iota(jnp.int32, (SUBLANES, h.shape[1]), 0) < 1
            keep = taps_back * SUBLANES
            prev = carry[:, cols]
            carry[:, cols] = h[tm - keep:, :]
            wrap = [jnp.where(first_rows,
                              pltpu.roll(prev[d * SUBLANES:(d + 1) * SUBLANES, :], 1, 0),
                              pltpu.roll(h[tm - keep + d * SUBLANES:tm - keep + (d + 1) * SUBLANES, :], 1, 0))
                    for d in range(taps_back)]
            out = cb_ref[:, cols] + cw_ref[taps_back:taps_back + 1, cols] * h
            for d in range(1, FFN_CONV):
                back = jnp.concatenate(wrap[taps_back - d:] + [h[:tm - d * SUBLANES, :]], axis=0)
                out = out + cw_ref[taps_back - d:taps_back - d + 1, cols] * back
            return out

        widths = [fc] * (D_FF // fc)
        starts = [sum(widths[:j]) for j in range(len(widths))]
        acc = jnp.zeros((tm, D_MODEL), _F32)
        h_gate, h_lin = up(starts[0], widths[0])
        act = None
        for j in range(len(widths)):
            if j + 1 < len(widths):
                next_gate, next_lin = up(starts[j + 1], widths[j + 1])
            if act is not None:
                acc = acc + _dot(act, wdn_bf[j - 1])
            act = (_gelu_x2(conv(h_gate, starts[j])) * conv(h_lin, D_FF + starts[j])).astype(_BF16)
            h_gate, h_lin = next_gate, next_lin
        acc = acc + _dot(act, wdn_bf[len(widths) - 1])
        y = _layer_norm(DN_ALPHA * x + acc, g_ref[...], b_ref[...])
        for n in range(lane_tiles):
            relay[n] = y[:, n * LANES:(n + 1) * LANES]
        o_ref[...] = jnp.concatenate(
            [jnp.concatenate([relay[n, pl.ds(permuted_start(g), SUBLANES, stride=SUBLANES), :]
                              for g in range(n_groups)], axis=0)
             for n in range(lane_tiles)], axis=1)


def _ffn_layer(x2d, batch, seq, layer, wup_all, cw, cb, wdn_all, g, b):
    tm = TOKEN_BLOCK
    fc = FF_CHUNK
    blocks_per_seq = seq // tm
    n_blocks = batch * blocks_per_seq
    n_up, n_dn = 2 * D_FF // fc, D_FF // fc
    tok = pl.BlockSpec((tm, D_MODEL), lambda s: (jnp.maximum(s - n_up, 0), 0))
    body = functools.partial(_ffn_kernel, n_cast=n_up, blocks_per_seq=blocks_per_seq)
    return pl.pallas_call(
        body,
        grid=(n_up + n_blocks,),
        in_specs=[tok,
                  pl.BlockSpec((None, D_MODEL, fc), lambda s: (layer, 0, jnp.minimum(s, n_up - 1))),
                  pl.BlockSpec((None, fc, D_MODEL), lambda s: (layer, jnp.minimum(s, n_dn - 1), 0)),
                  _const_spec(cw.shape), _const_spec(cb.shape),
                  _const_spec(g.shape), _const_spec(b.shape)],
        out_specs=tok,
        out_shape=jax.ShapeDtypeStruct(x2d.shape, _F32),
        scratch_shapes=[pltpu.VMEM((n_up, D_MODEL, fc), _BF16),
                        pltpu.VMEM((n_dn, fc, D_MODEL), _BF16),
                        pltpu.VMEM(((FFN_CONV - 1) * SUBLANES, 2 * D_FF), _F32),
                        pltpu.VMEM((D_MODEL // LANES, tm, LANES), _F32)],
        compiler_params=_params(("arbitrary",)),
        name="conv_ffn",
    )(x2d, wup_all, wdn_all, cw, cb, g, b)


def _qkv_kernel(x_ref, w_ref, o_ref):
    res = _dot(x_ref[...].astype(_BF16), w_ref[...])
    for n in range(3 * N_HEADS):
        part = res[:, n * HEAD_DIM:(n + 1) * HEAD_DIM]
        if n >= 2 * N_HEADS:
            part = part * (HEAD_DIM ** -0.5)
        o_ref[n] = part.astype(_BF16)


def _qkv_proj(x2d, w):
    tm = TOKEN_BLOCK
    tokens = x2d.shape[0]
    return pl.pallas_call(
        _qkv_kernel,
        grid=(tokens // tm,),
        in_specs=[pl.BlockSpec((tm, D_MODEL), lambda i: (i, 0)), _const_spec(w.shape)],
        out_specs=pl.BlockSpec((3 * N_HEADS, tm, HEAD_DIM), lambda i: (0, i, 0)),
        out_shape=jax.ShapeDtypeStruct((3 * N_HEADS, tokens, HEAD_DIM), _BF16),
        compiler_params=_params(("arbitrary",)),
        name="kvq_proj",
    )(x2d, w)


def _attn_kernel(q_ref, k_ref, v_ref, x_ref, wo_ref, g_ref, b_ref, y_ref, o_acc, tail_acc, o_done,
                 *, n_blocks):
    tq, tk = ATTN_Q_BLOCK, ATTN_K_BLOCK
    i = pl.program_id(1)
    row = lax.broadcasted_iota(jnp.int32, (tq, tk), 0)
    col = lax.broadcasted_iota(jnp.int32, (tq, tk), 1)
    later = (row > col).astype(_BF16)
    causal = col < row

    def logits(h, j):
        k = k_ref[h, pl.ds(pl.multiple_of(j * tk, tk), tk), :]
        z = lax.dot_general(q_ref[h], k, (((1,), (1,)), ((), ())), preferred_element_type=_F32)
        log_beta = jnp.minimum(z, 0.0) - jnp.log(1.0 + jnp.exp(-jnp.abs(z)))
        return log_beta, log_beta - z

    def values(h, j):
        return v_ref[h, pl.ds(pl.multiple_of(j * tk, tk), tk), :]

    def earlier(h, j, o, tail):
        log_beta, log_1m = logits(h, j)
        suffix = _dot(log_1m.astype(_BF16), later) + tail
        w = jnp.exp(log_beta + suffix)
        return (o + _dot(w.astype(_BF16), values(h, j)),
                tail + jnp.sum(log_1m, axis=1, keepdims=True))

    def stage_logits(h, j, on_diagonal):
        log_beta, log_1m = logits(h, j)
        if on_diagonal:
            log_1m = jnp.where(causal, log_1m, 0.0)
        return log_beta, log_1m.astype(_BF16), jnp.sum(log_1m, axis=1, keepdims=True)

    def stage_weights(h, j, on_diagonal, log_beta, suffix, tail):
        x = log_beta + suffix if tail is None else log_beta + suffix + tail
        w = jnp.exp(x)
        if on_diagonal:
            w = jnp.where(causal, w, 0.0)
        return _dot(w.astype(_BF16), values(h, j))

    def run_tiles(tiles, side_jobs=()):
        first, suffix, out, side = {}, {}, {}, []
        n_iter = len(tiles) + 2
        for t in range(n_iter):
            side += [job() for job in
                     side_jobs[t * len(side_jobs) // n_iter:(t + 1) * len(side_jobs) // n_iter]]
            if t < len(tiles):
                h, j, diag, _ = tiles[t]
                first[t] = stage_logits(h, j, diag)
            if 0 <= t - 1 < len(tiles):
                suffix[t - 1] = _dot(first[t - 1][1], later)
            if 0 <= t - 2 < len(tiles):
                h, j, diag, prev = tiles[t - 2]
                tail = None if prev is None else first[prev][2]
                out[t - 2] = stage_weights(h, j, diag, first[t - 2][0], suffix[t - 2], tail)
        return [out[t] for t in range(len(tiles))], [first[t][2] for t in range(len(tiles))], side

    def out_proj_jobs(o):
        def tile(t):
            return _dot(o, wo_ref[:, t * tk:(t + 1) * tk])
        return [functools.partial(tile, t) for t in range(D_MODEL // tk)]

    def finish_previous(mix_tiles):
        mix = jnp.concatenate(mix_tiles, axis=1)
        y_ref[...] = _layer_norm(DN_ALPHA * x_ref[...] + mix, g_ref[...], b_ref[...])

    def park_outputs():
        o_done[...] = jnp.concatenate([o_acc[h] for h in range(N_HEADS)], axis=1).astype(_BF16)

    def alive(tail):
        return jnp.max(tail) >= EXP_ZERO_CUTOFF

    @pl.when(i == 0)
    def _():
        outs, _, _ = run_tiles([(h, i, True, None) for h in range(N_HEADS)])
        for h in range(N_HEADS):
            o_acc[h] = outs[h]
        park_outputs()

    @pl.when(jnp.logical_and(i > 0, i < n_blocks))
    def _():
        tiles = []
        for h in range(N_HEADS):
            tiles += [(h, i, True, None), (h, i - 1, False, 2 * h)]
        outs, sums, mix = run_tiles(tiles, out_proj_jobs(o_done[...]))
        finish_previous(mix)
        worst = jnp.full((tq, 1), -jnp.inf, _F32)
        for h in range(N_HEADS):
            tail = sums[2 * h] + sums[2 * h + 1]
            o_acc[h] = outs[2 * h] + outs[2 * h + 1]
            tail_acc[h] = tail
            worst = jnp.maximum(worst, tail)

        @pl.when(jnp.logical_and(i > 1, alive(worst)))
        def _():
            def head(h, _):
                def cond(state):
                    j, _, _, live = state
                    return jnp.logical_and(j >= 0, live > 0)

                def body(state):
                    j, o, tail, _ = state
                    o, tail = earlier(h, j, o, tail)
                    return j - 1, o, tail, alive(tail).astype(jnp.int32)

                tail = tail_acc[h]
                state = (i - 2, o_acc[h], tail, alive(tail).astype(jnp.int32))
                o_acc[h] = lax.while_loop(cond, body, state)[1]
                return 0

            lax.fori_loop(0, N_HEADS, head, 0)

        park_outputs()

    @pl.when(i == n_blocks)
    def _():
        finish_previous([job() for job in out_proj_jobs(o_done[...])])


def _attention_layer(kvq, x2d, batch, seq, wo, g, b):
    tq = ATTN_Q_BLOCK
    nq = seq // tq
    tok_prev = pl.BlockSpec((tq, D_MODEL), lambda bi, i: (bi * nq + jnp.maximum(i - 1, 0), 0))

    def kv_batch(bi, i):
        return jnp.minimum(bi + i // nq, batch - 1)

    return pl.pallas_call(
        functools.partial(_attn_kernel, n_blocks=nq),
        grid=(batch, nq + 1),
        in_specs=[pl.BlockSpec((N_HEADS, tq, HEAD_DIM),
                               lambda bi, i: (2, bi * nq + jnp.minimum(i, nq - 1), 0)),
                  pl.BlockSpec((N_HEADS, seq, HEAD_DIM), lambda bi, i: (0, kv_batch(bi, i), 0)),
                  pl.BlockSpec((N_HEADS, seq, HEAD_DIM), lambda bi, i: (1, kv_batch(bi, i), 0)),
                  tok_prev, _const_spec(wo.shape), _const_spec(g.shape), _const_spec(b.shape)],
        out_specs=tok_prev,
        out_shape=jax.ShapeDtypeStruct(x2d.shape, _F32),
        scratch_shapes=[pltpu.VMEM((N_HEADS, tq, HEAD_DIM), _F32),
                        pltpu.VMEM((N_HEADS, tq, 1), _F32),
                        pltpu.VMEM((tq, N_HEADS * HEAD_DIM), _BF16)],
        compiler_params=_params(("arbitrary", "arbitrary")),
        name="stickbreak_attn",
    )(kvq, kvq, kvq, x2d, wo, g, b)


def kernel(x, lru_w_in, lru_b_in, lru_conv_w, lru_conv_b, lru_w_gates, lru_b_gates, lru_a_param, lru_w_out, lru_b_out, kv_w, attn_w_q, attn_w_out, ffn_w_up, ffn_conv_w, ffn_conv_b, ffn_w_down, ln_g, ln_b):
    batch, seq, d = x.shape
    assert d == D_MODEL and seq % TOKEN_BLOCK == 0 and seq % ATTN_Q_BLOCK == 0
    assert lru_w_in.shape[0] == 1 and attn_w_q.shape[0] == 1 and ffn_w_up.shape[0] == DEPTH
    row = lambda v: v.reshape(1, -1)
    h = x.reshape(batch * seq, d)

    half_lin = jnp.concatenate([jnp.ones((D_FF,), _F32), jnp.full((D_FF,), 0.5, _F32)])

    def ffn(h, layer):
        return _ffn_layer(h, batch, seq, layer, ffn_w_up, ffn_conv_w[layer] * half_lin,
                          row(ffn_conv_b[layer] * half_lin), ffn_w_down,
                          row(ln_g[layer, 1]), row(ln_b[layer, 1]))

    h = _lru_layer(h, batch, seq, lru_w_in[0].astype(_BF16), row(lru_b_in[0]), lru_conv_w[0],
                   row(lru_conv_b[0]), (0.5 * lru_w_gates[0]).astype(_BF16),
                   (0.5 * lru_b_gates[0]).reshape(LRU_HEADS, 1, 2 * LRU_BLOCK), row(lru_a_param[0]),
                   (0.5 * lru_w_out[0]).astype(_BF16), row(lru_b_out[0]), row(ln_g[0, 0]), row(ln_b[0, 0]))
    h = ffn(h, 0)
    kvq = _qkv_proj(h, jnp.concatenate([kv_w, attn_w_q[0]], axis=1).astype(_BF16))
    h = _attention_layer(kvq, h, batch, seq, attn_w_out[0].astype(_BF16),
                         row(ln_g[1, 0]), row(ln_b[1, 0]))
    h = ffn(h, 1)
    return h.reshape(batch, seq, d)
```

```python
import functools
import math

import jax
import jax.numpy as jnp
from jax import lax
from jax.experimental import pallas as pl
from jax.experimental.pallas import tpu as pltpu

D_MODEL = 1024
DEPTH = 2
LRU_WIDTH = D_MODEL
LRU_HEADS = 8
LRU_BLOCK = LRU_WIDTH // LRU_HEADS
LRU_CONV = 4
LRU_C = 8.0
N_HEADS = 8
HEAD_DIM = D_MODEL // N_HEADS
D_FF = 3 * D_MODEL
FFN_CONV = 3
DN_ALPHA = (2 * DEPTH) ** 0.25
LN_EPS = 1e-5

SUBLANES = 8
LANES = 128
TOKEN_BLOCK = 512
LRU_TOKEN_BLOCK = 1024
FF_CHUNK = 512
ATTN_Q_BLOCK = 256
ATTN_K_BLOCK = 256
EXP_ZERO_CUTOFF = -104.0
VMEM_LIMIT = 52 * 1024 * 1024

_BF16 = jnp.bfloat16
_F32 = jnp.float32


def _dot(a, b):
    return jnp.dot(a, b, preferred_element_type=_F32)


def _gelu_x2(x):
    c = math.sqrt(2.0 / math.pi)
    t = jnp.tanh(x * (c + (c * 0.044715) * (x * x)))
    return x + x * t


def _layer_norm(y, g, b):
    mu = jnp.mean(y, axis=-1, keepdims=True)
    yc = y - mu
    var = jnp.mean(yc * yc, axis=-1, keepdims=True)
    return yc * lax.rsqrt(var + LN_EPS) * g + b


def _const_spec(shape):
    zeros = (0,) * len(shape)
    return pl.BlockSpec(shape, lambda *_: zeros, pipeline_mode=pl.Buffered(1))


def _params(sem):
    return pltpu.CompilerParams(dimension_semantics=sem, vmem_limit_bytes=VMEM_LIMIT)


def _lru_kernel(x_ref, win_ref, bin_ref, cw_ref, cb_ref, wg_ref, bg_ref, ap_ref,
                wout_ref, bout_ref, g_ref, b_ref, fup_ref, fdn_ref, o_ref, fup_bf_ref, fdn_bf_ref,
                xp_buf, x_tail, a_buf, u_buf, h_buf, h_carry):
    fup_bf_ref[...] = fup_ref[...].astype(_BF16)
    fdn_bf_ref[...] = fdn_ref[...].astype(_BF16)
    tm = x_ref.shape[0]
    W = LRU_WIDTH
    s = pl.program_id(1)

    @pl.when(s == 0)
    def _():
        x_tail[...] = jnp.zeros(x_tail.shape, _F32)
        h_carry[...] = jnp.zeros((SUBLANES, W), _F32)

    hm = tm // 2
    nt = 2 * LRU_BLOCK
    half_c_sp = (-0.5 * LRU_C) * jnp.logaddexp(-ap_ref[...], 0.0)
    sub = lax.broadcasted_iota(jnp.int32, (SUBLANES, W), 0)
    row = lax.broadcasted_iota(jnp.int32, (hm, LRU_BLOCK), 0)

    x_halves = [x_ref[k * hm:(k + 1) * hm, :].astype(_BF16) for k in range(2)]

    def in_proj_tile(k, t):
        return _dot(x_halves[k], win_ref[:, t * nt:(t + 1) * nt]) + bin_ref[:, t * nt:(t + 1) * nt]

    seg = hm // SUBLANES
    per_seg = seg // SUBLANES

    def permuted_start(g):
        return SUBLANES * SUBLANES * (g % per_seg) + g // per_seg

    def recurrence(k, xr_tiles, h_prev, side_jobs):
        results = []
        first_rows = lax.broadcasted_iota(jnp.int32, (SUBLANES, LRU_BLOCK), 0) < 1
        for n in range(LRU_HEADS):
            results += [job() for job in side_jobs[n]]
            cols = slice(n * LRU_BLOCK, (n + 1) * LRU_BLOCK)
            tile = xr_tiles[n // 2]
            xr = tile[:, (n % 2) * LRU_BLOCK:(n % 2 + 1) * LRU_BLOCK]
            for g in range(hm // SUBLANES):
                xp_buf[n, pl.ds(permuted_start(g), SUBLANES, stride=SUBLANES), :] = (
                    xr[g * SUBLANES:(g + 1) * SUBLANES, :])
            xp = xp_buf[n]
            prev = x_tail[n]
            keep = (LRU_CONV - 1) * SUBLANES
            x_tail[n] = xp[hm - keep:, :]
            wrap = [jnp.where(first_rows,
                              pltpu.roll(prev[d * SUBLANES:(d + 1) * SUBLANES, :], 1, 0),
                              pltpu.roll(xp[hm - keep + d * SUBLANES:hm - keep + (d + 1) * SUBLANES, :], 1, 0))
                    for d in range(LRU_CONV - 1)]
            xn = cb_ref[:, cols] + cw_ref[LRU_CONV - 1:LRU_CONV, cols] * xp
            for d in range(1, LRU_CONV):
                back = jnp.concatenate(wrap[LRU_CONV - 1 - d:] + [xp[:hm - d * SUBLANES, :]], axis=0)
                xn = xn + cw_ref[LRU_CONV - 1 - d:LRU_CONV - d, cols] * back
            t = jnp.tanh(_dot(xn.astype(_BF16), wg_ref[n]) + bg_ref[n])
            gate_i = 0.5 + 0.5 * t[:, :LRU_BLOCK]
            a = jnp.exp(half_c_sp[:, cols] + half_c_sp[:, cols] * t[:, LRU_BLOCK:])
            m = jnp.maximum(1.0 - a * a, 0.0)
            mult = jnp.where(m > 0.0, m * lax.rsqrt(m), 0.0)
            if k == 0:
                mult = jnp.where(jnp.logical_and(row == 0, s == 0), 1.0, mult)
            a_buf[k * hm:(k + 1) * hm, cols] = a
            u_buf[k * hm:(k + 1) * hm, cols] = mult * gate_i * xn

        def load(v):
            r = pl.multiple_of(k * hm + v * SUBLANES, SUBLANES)
            return a_buf[pl.ds(r, SUBLANES), :], u_buf[pl.ds(r, SUBLANES), :], r

        def pass1(v, state):
            h, p = state
            a, u, _ = load(v)
            return a * h + u, p * a

        end, prod = lax.fori_loop(0, seg, pass1, (jnp.zeros((SUBLANES, W), _F32),
                                                  jnp.ones((SUBLANES, W), _F32)))
        for d in (1, 2, 4):
            ok = sub >= d
            end = end + prod * jnp.where(ok, pltpu.roll(end, d, 0), 0.0)
            prod = prod * jnp.where(ok, pltpu.roll(prod, d, 0), 1.0)
        seg_out = end + prod * h_prev
        seg_in = jnp.where(sub < 1, h_prev, pltpu.roll(seg_out, 1, 0))

        def pass2(v, h):
            a, u, r = load(v)
            h = a * h + u
            for n in range(LRU_HEADS):
                h_buf[n, pl.ds(r, SUBLANES), :] = h[:, n * LRU_BLOCK:(n + 1) * LRU_BLOCK]
            return h

        lax.fori_loop(0, seg, pass2, seg_in)
        return jnp.broadcast_to(seg_out[SUBLANES - 1:SUBLANES, :], (SUBLANES, W)), results

    def gated(k, y_tiles):
        y_br = _gelu_x2(jnp.concatenate(y_tiles, axis=1))
        h = jnp.concatenate(
            [jnp.concatenate(
                [h_buf[n, pl.ds(k * hm + permuted_start(g), SUBLANES, stride=SUBLANES), :]
                 for g in range(hm // SUBLANES)], axis=0)
             for n in range(LRU_HEADS)], axis=1)
        return (h * y_br).astype(_BF16)

    def out_proj_tile(hy, t):
        return _dot(hy, wout_ref[:, t * nt:(t + 1) * nt]) + bout_ref[:, t * nt:(t + 1) * nt]

    def finish(k, mix_tiles):
        rows = slice(k * hm, (k + 1) * hm)
        mix = jnp.concatenate(mix_tiles, axis=1)
        o_ref[rows, :] = _layer_norm(DN_ALPHA * x_ref[rows, :] + mix, g_ref[...], b_ref[...])

    half = W // nt
    job = functools.partial
    xr0 = [in_proj_tile(0, half + t) for t in range(half)]
    jobs = ([[job(in_proj_tile, 0, t), job(in_proj_tile, 1, half + t)] for t in range(half)]
            + [[job(in_proj_tile, 1, t)] for t in range(half)])
    h_mid, res = recurrence(0, xr0, h_carry[...], jobs)
    y0, xr1, y1 = res[0:2 * half:2], res[1:2 * half:2], res[2 * half:]
    hy0 = gated(0, y0)
    n_out = D_MODEL // nt
    jobs = [[job(out_proj_tile, hy0, t)] for t in range(n_out)] + [[]] * (LRU_HEADS - n_out)
    h_last, mix0 = recurrence(1, xr1, h_mid, jobs)
    h_carry[...] = h_last
    finish(0, mix0)
    hy1 = gated(1, y1)
    finish(1, [out_proj_tile(hy1, t) for t in range(n_out)])


def _cast_specs(n_steps, layer, step_of):
    up_rows, dn_rows = D_MODEL // n_steps, D_FF // n_steps
    ins = [pl.BlockSpec((None, up_rows, 2 * D_FF), lambda *ids: (layer, step_of(*ids), 0)),
           pl.BlockSpec((None, dn_rows, D_MODEL), lambda *ids: (layer, step_of(*ids), 0))]
    outs = [pl.BlockSpec((up_rows, 2 * D_FF), lambda *ids: (step_of(*ids), 0)),
            pl.BlockSpec((dn_rows, D_MODEL), lambda *ids: (step_of(*ids), 0))]
    shapes = [jax.ShapeDtypeStruct((D_MODEL, 2 * D_FF), _BF16),
              jax.ShapeDtypeStruct((D_FF, D_MODEL), _BF16)]
    return ins, outs, shapes


def _lru_layer(x2d, batch, seq, win, b_in, cw, cb, wg, bg, ap, wout, bout, g, b, ffn_w_up, ffn_w_down):
    tm = LRU_TOKEN_BLOCK
    W = LRU_WIDTH
    ns = seq // tm
    tok = pl.BlockSpec((tm, D_MODEL), lambda bi, si: (bi * ns + si, 0))
    cast_in, cast_out, cast_shapes = _cast_specs(batch * ns, 0, lambda bi, si: bi * ns + si)
    return pl.pallas_call(
        _lru_kernel,
        grid=(batch, ns),
        in_specs=[tok, _const_spec(win.shape), _const_spec(b_in.shape), _const_spec(cw.shape),
                  _const_spec(cb.shape), _const_spec(wg.shape), _const_spec(bg.shape),
                  _const_spec(ap.shape), _const_spec(wout.shape), _const_spec(bout.shape),
                  _const_spec(g.shape), _const_spec(b.shape)] + cast_in,
        out_specs=[tok] + cast_out,
        out_shape=[jax.ShapeDtypeStruct(x2d.shape, _F32)] + cast_shapes,
        scratch_shapes=[pltpu.VMEM((LRU_HEADS, tm // 2, LRU_BLOCK), _F32),
                        pltpu.VMEM((LRU_HEADS, (LRU_CONV - 1) * SUBLANES, LRU_BLOCK), _F32),
                        pltpu.VMEM((tm, W), _F32), pltpu.VMEM((tm, W), _F32),
                        pltpu.VMEM((LRU_HEADS, tm, LRU_BLOCK), _F32),
                        pltpu.VMEM((SUBLANES, W), _F32)],
        compiler_params=_params(("arbitrary", "arbitrary")),
        name="lru_layer",
    )(x2d, win, b_in, cw, cb, wg, bg, ap, wout, bout, g, b, ffn_w_up, ffn_w_down)


def _ffn_kernel(x_ref, wup_ref, cw_ref, cb_ref, wdn_ref, g_ref, b_ref, o_ref, carry, relay):
    tm = x_ref.shape[0]
    fc = FF_CHUNK
    s = pl.program_id(1)
    n_groups = tm // SUBLANES
    lane_tiles = D_MODEL // LANES
    taps_back = FFN_CONV - 1

    @pl.when(s == 0)
    def _():
        carry[...] = jnp.zeros(carry.shape, _F32)

    per_seg = n_groups // SUBLANES

    def permuted_start(g):
        return SUBLANES * SUBLANES * (g % per_seg) + g // per_seg

    x_nat = x_ref[...]
    for n in range(lane_tiles):
        for g in range(n_groups):
            relay[n, pl.ds(permuted_start(g), SUBLANES, stride=SUBLANES), :] = (
                x_nat[g * SUBLANES:(g + 1) * SUBLANES, n * LANES:(n + 1) * LANES])
    x = jnp.concatenate([relay[n] for n in range(lane_tiles)], axis=1)
    xb = x.astype(_BF16)

    def up(c0, width):
        return (_dot(xb, wup_ref[:, c0:c0 + width]),
                _dot(xb, wup_ref[:, D_FF + c0:D_FF + c0 + width]))

    def conv(h, col0):
        cols = slice(col0, col0 + h.shape[1])
        first_rows = lax.broadcasted_iota(jnp.int32, (SUBLANES, h.shape[1]), 0) < 1
        keep = taps_back * SUBLANES
        prev = carry[:, cols]
        carry[:, cols] = h[tm - keep:, :]
        wrap = [jnp.where(first_rows,
                          pltpu.roll(prev[d * SUBLANES:(d + 1) * SUBLANES, :], 1, 0),
                          pltpu.roll(h[tm - keep + d * SUBLANES:tm - keep + (d + 1) * SUBLANES, :], 1, 0))
                for d in range(taps_back)]
        out = cb_ref[:, cols] + cw_ref[taps_back:taps_back + 1, cols] * h
        for d in range(1, FFN_CONV):
            back = jnp.concatenate(wrap[taps_back - d:] + [h[:tm - d * SUBLANES, :]], axis=0)
            out = out + cw_ref[taps_back - d:taps_back - d + 1, cols] * back
        return out

    widths = [fc] * (D_FF // fc)
    starts = [sum(widths[:j]) for j in range(len(widths))]
    acc = jnp.zeros((tm, D_MODEL), _F32)
    h_gate, h_lin = up(starts[0], widths[0])
    act = None
    for j in range(len(widths)):
        if j + 1 < len(widths):
            next_gate, next_lin = up(starts[j + 1], widths[j + 1])
        if act is not None:
            acc = acc + _dot(act, wdn_ref[starts[j - 1]:starts[j], :])
        act = (_gelu_x2(conv(h_gate, starts[j])) * conv(h_lin, D_FF + starts[j])).astype(_BF16)
        h_gate, h_lin = next_gate, next_lin
    acc = acc + _dot(act, wdn_ref[starts[-1]:, :])
    y = _layer_norm(DN_ALPHA * x + acc, g_ref[...], b_ref[...])
    for n in range(lane_tiles):
        relay[n] = y[:, n * LANES:(n + 1) * LANES]
    o_ref[...] = jnp.concatenate(
        [jnp.concatenate([relay[n, pl.ds(permuted_start(g), SUBLANES, stride=SUBLANES), :]
                          for g in range(n_groups)], axis=0)
         for n in range(lane_tiles)], axis=1)


def _ffn_layer(x2d, batch, seq, wup, cw, cb, wdn, g, b):
    tm = TOKEN_BLOCK
    ns = seq // tm
    tok = pl.BlockSpec((tm, D_MODEL), lambda bi, si: (bi * ns + si, 0))
    return pl.pallas_call(
        _ffn_kernel,
        grid=(batch, ns),
        in_specs=[tok, _const_spec(wup.shape), _const_spec(cw.shape), _const_spec(cb.shape),
                  _const_spec(wdn.shape), _const_spec(g.shape), _const_spec(b.shape)],
        out_specs=tok,
        out_shape=jax.ShapeDtypeStruct(x2d.shape, _F32),
        scratch_shapes=[pltpu.VMEM(((FFN_CONV - 1) * SUBLANES, 2 * D_FF), _F32),
                        pltpu.VMEM((D_MODEL // LANES, tm, LANES), _F32)],
        compiler_params=_params(("arbitrary", "arbitrary")),
        name="conv_ffn",
    )(x2d, wup, cw, cb, wdn, g, b)


def _qkv_kernel(x_ref, w_ref, fup_ref, fdn_ref, o_ref, fup_bf_ref, fdn_bf_ref):
    fup_bf_ref[...] = fup_ref[...].astype(_BF16)
    fdn_bf_ref[...] = fdn_ref[...].astype(_BF16)
    res = _dot(x_ref[...].astype(_BF16), w_ref[...])
    for n in range(3 * N_HEADS):
        part = res[:, n * HEAD_DIM:(n + 1) * HEAD_DIM]
        if n >= 2 * N_HEADS:
            part = part * (HEAD_DIM ** -0.5)
        o_ref[n] = part.astype(_BF16)


def _qkv_proj(x2d, w, ffn_w_up, ffn_w_down):
    tm = TOKEN_BLOCK
    tokens = x2d.shape[0]
    cast_in, cast_out, cast_shapes = _cast_specs(tokens // tm, DEPTH - 1, lambda i: i)
    return pl.pallas_call(
        _qkv_kernel,
        grid=(tokens // tm,),
        in_specs=[pl.BlockSpec((tm, D_MODEL), lambda i: (i, 0)), _const_spec(w.shape)] + cast_in,
        out_specs=[pl.BlockSpec((3 * N_HEADS, tm, HEAD_DIM), lambda i: (0, i, 0))] + cast_out,
        out_shape=[jax.ShapeDtypeStruct((3 * N_HEADS, tokens, HEAD_DIM), _BF16)] + cast_shapes,
        compiler_params=_params(("arbitrary",)),
        name="kvq_proj",
    )(x2d, w, ffn_w_up, ffn_w_down)


def _attn_kernel(q_ref, k_ref, v_ref, x_ref, wo_ref, g_ref, b_ref, y_ref, o_acc, tail_acc, o_done,
                 *, n_blocks):
    tq, tk = ATTN_Q_BLOCK, ATTN_K_BLOCK
    i = pl.program_id(1)
    row = lax.broadcasted_iota(jnp.int32, (tq, tk), 0)
    col = lax.broadcasted_iota(jnp.int32, (tq, tk), 1)
    later = (row > col).astype(_BF16)
    causal = col < row

    def logits(h, j):
        k = k_ref[h, pl.ds(pl.multiple_of(j * tk, tk), tk), :]
        z = lax.dot_general(q_ref[h], k, (((1,), (1,)), ((), ())), preferred_element_type=_F32)
        log_beta = jnp.minimum(z, 0.0) - jnp.log(1.0 + jnp.exp(-jnp.abs(z)))
        return log_beta, log_beta - z

    def values(h, j):
        return v_ref[h, pl.ds(pl.multiple_of(j * tk, tk), tk), :]

    def earlier(h, j, o, tail):
        log_beta, log_1m = logits(h, j)
        suffix = _dot(log_1m.astype(_BF16), later) + tail
        w = jnp.exp(log_beta + suffix)
        return (o + _dot(w.astype(_BF16), values(h, j)),
                tail + jnp.sum(log_1m, axis=1, keepdims=True))

    def stage_logits(h, j, on_diagonal):
        log_beta, log_1m = logits(h, j)
        if on_diagonal:
            log_1m = jnp.where(causal, log_1m, 0.0)
        return log_beta, log_1m.astype(_BF16), jnp.sum(log_1m, axis=1, keepdims=True)

    def stage_weights(h, j, on_diagonal, log_beta, suffix, tail):
        x = log_beta + suffix if tail is None else log_beta + suffix + tail
        w = jnp.exp(x)
        if on_diagonal:
            w = jnp.where(causal, w, 0.0)
        return _dot(w.astype(_BF16), values(h, j))

    def run_tiles(tiles, side_jobs=()):
        first, suffix, out, side = {}, {}, {}, []
        n_iter = len(tiles) + 2
        for t in range(n_iter):
            side += [job() for job in
                     side_jobs[t * len(side_jobs) // n_iter:(t + 1) * len(side_jobs) // n_iter]]
            if t < len(tiles):
                h, j, diag, _ = tiles[t]
                first[t] = stage_logits(h, j, diag)
            if 0 <= t - 1 < len(tiles):
                suffix[t - 1] = _dot(first[t - 1][1], later)
            if 0 <= t - 2 < len(tiles):
                h, j, diag, prev = tiles[t - 2]
                tail = None if prev is None else first[prev][2]
                out[t - 2] = stage_weights(h, j, diag, first[t - 2][0], suffix[t - 2], tail)
        return [out[t] for t in range(len(tiles))], [first[t][2] for t in range(len(tiles))], side

    def out_proj_jobs(o):
        def tile(t):
            return _dot(o, wo_ref[:, t * tk:(t + 1) * tk])
        return [functools.partial(tile, t) for t in range(D_MODEL // tk)]

    def finish_previous(mix_tiles):
        mix = jnp.concatenate(mix_tiles, axis=1)
        y_ref[...] = _layer_norm(DN_ALPHA * x_ref[...] + mix, g_ref[...], b_ref[...])

    def park_outputs():
        o_done[...] = jnp.concatenate([o_acc[h] for h in range(N_HEADS)], axis=1).astype(_BF16)

    def alive(tail):
        return jnp.max(tail) >= EXP_ZERO_CUTOFF

    @pl.when(i == 0)
    def _():
        outs, _, _ = run_tiles([(h, i, True, None) for h in range(N_HEADS)])
        for h in range(N_HEADS):
            o_acc[h] = outs[h]
        park_outputs()

    @pl.when(jnp.logical_and(i > 0, i < n_blocks))
    def _():
        tiles = []
        for h in range(N_HEADS):
            tiles += [(h, i, True, None), (h, i - 1, False, 2 * h)]
        outs, sums, mix = run_tiles(tiles, out_proj_jobs(o_done[...]))
        finish_previous(mix)
        worst = jnp.full((tq, 1), -jnp.inf, _F32)
        for h in range(N_HEADS):
            tail = sums[2 * h] + sums[2 * h + 1]
            o_acc[h] = outs[2 * h] + outs[2 * h + 1]
            tail_acc[h] = tail
            worst = jnp.maximum(worst, tail)

        @pl.when(jnp.logical_and(i > 1, alive(worst)))
        def _():
            def head(h, _):
                def cond(state):
                    j, _, _, live = state
                    return jnp.logical_and(j >= 0, live > 0)

                def body(state):
                    j, o, tail, _ = state
                    o, tail = earlier(h, j, o, tail)
                    return j - 1, o, tail, alive(tail).astype(jnp.int32)

                tail = tail_acc[h]
                state = (i - 2, o_acc[h], tail, alive(tail).astype(jnp.int32))
                o_acc[h] = lax.while_loop(cond, body, state)[1]
                return 0

            lax.fori_loop(0, N_HEADS, head, 0)

        park_outputs()

    @pl.when(i == n_blocks)
    def _():
        finish_previous([job() for job in out_proj_jobs(o_done[...])])


def _attention_layer(kvq, x2d, batch, seq, wo, g, b):
    tq = ATTN_Q_BLOCK
    nq = seq // tq
    tok_prev = pl.BlockSpec((tq, D_MODEL), lambda bi, i: (bi * nq + jnp.maximum(i - 1, 0), 0))

    def kv_batch(bi, i):
        return jnp.minimum(bi + i // nq, batch - 1)

    return pl.pallas_call(
        functools.partial(_attn_kernel, n_blocks=nq),
        grid=(batch, nq + 1),
        in_specs=[pl.BlockSpec((N_HEADS, tq, HEAD_DIM),
                               lambda bi, i: (2, bi * nq + jnp.minimum(i, nq - 1), 0)),
                  pl.BlockSpec((N_HEADS, seq, HEAD_DIM), lambda bi, i: (0, kv_batch(bi, i), 0)),
                  pl.BlockSpec((N_HEADS, seq, HEAD_DIM), lambda bi, i: (1, kv_batch(bi, i), 0)),
                  tok_prev, _const_spec(wo.shape), _const_spec(g.shape), _const_spec(b.shape)],
        out_specs=tok_prev,
        out_shape=jax.ShapeDtypeStruct(x2d.shape, _F32),
        scratch_shapes=[pltpu.VMEM((N_HEADS, tq, HEAD_DIM), _F32),
                        pltpu.VMEM((N_HEADS, tq, 1), _F32),
                        pltpu.VMEM((tq, N_HEADS * HEAD_DIM), _BF16)],
        compiler_params=_params(("arbitrary", "arbitrary")),
        name="stickbreak_attn",
    )(kvq, kvq, kvq, x2d, wo, g, b)


def kernel(x, lru_w_in, lru_b_in, lru_conv_w, lru_conv_b, lru_w_gates, lru_b_gates, lru_a_param, lru_w_out, lru_b_out, kv_w, attn_w_q, attn_w_out, ffn_w_up, ffn_conv_w, ffn_conv_b, ffn_w_down, ln_g, ln_b):
    batch, seq, d = x.shape
    assert d == D_MODEL and seq % TOKEN_BLOCK == 0 and seq % ATTN_Q_BLOCK == 0
    assert lru_w_in.shape[0] == 1 and attn_w_q.shape[0] == 1 and ffn_w_up.shape[0] == DEPTH
    row = lambda v: v.reshape(1, -1)
    h = x.reshape(batch * seq, d)

    half_lin = jnp.concatenate([jnp.ones((D_FF,), _F32), jnp.full((D_FF,), 0.5, _F32)])

    def ffn(h, layer, wup, wdn):
        return _ffn_layer(h, batch, seq, wup, ffn_conv_w[layer] * half_lin,
                          row(ffn_conv_b[layer] * half_lin), wdn,
                          row(ln_g[layer, 1]), row(ln_b[layer, 1]))

    h, wup, wdn = _lru_layer(h, batch, seq, lru_w_in[0].astype(_BF16), row(lru_b_in[0]), lru_conv_w[0],
                             row(lru_conv_b[0]), (0.5 * lru_w_gates[0]).astype(_BF16),
                             (0.5 * lru_b_gates[0]).reshape(LRU_HEADS, 1, 2 * LRU_BLOCK),
                             row(lru_a_param[0]), (0.5 * lru_w_out[0]).astype(_BF16), row(lru_b_out[0]),
                             row(ln_g[0, 0]), row(ln_b[0, 0]), ffn_w_up, ffn_w_down)
    h = ffn(h, 0, wup, wdn)
    kvq, wup, wdn = _qkv_proj(h, jnp.concatenate([kv_w, attn_w_q[0]], axis=1).astype(_BF16),
                              ffn_w_up, ffn_w_down)
    h = _attention_layer(kvq, h, batch, seq, attn_w_out[0].astype(_BF16),
                         row(ln_g[1, 0]), row(ln_b[1, 0]))
    h = ffn(h, 1, wup, wdn)
    return h.reshape(batch, seq, d)
```

```python
import functools
import math

import jax
import jax.numpy as jnp
from jax import lax
from jax.experimental import pallas as pl
from jax.experimental.pallas import tpu as pltpu

D_MODEL = 1024
DEPTH = 2
LRU_WIDTH = D_MODEL
LRU_HEADS = 8
LRU_BLOCK = LRU_WIDTH // LRU_HEADS
LRU_CONV = 4
LRU_C = 8.0
N_HEADS = 8
HEAD_DIM = D_MODEL // N_HEADS
D_FF = 3 * D_MODEL
FFN_CONV = 3
DN_ALPHA = (2 * DEPTH) ** 0.25
LN_EPS = 1e-5

SUBLANES = 8
LANES = 128
TOKEN_BLOCK = 512
LRU_TOKEN_BLOCK = 1024
FF_CHUNK = 512
ATTN_Q_BLOCK = 256
ATTN_K_BLOCK = 256
EXP_ZERO_CUTOFF = -104.0
VMEM_LIMIT = 52 * 1024 * 1024

_BF16 = jnp.bfloat16
_F32 = jnp.float32


def _dot(a, b):
    return jnp.dot(a, b, preferred_element_type=_F32)


def _gelu_x2(x):
    c = math.sqrt(2.0 / math.pi)
    t = jnp.tanh(x * (c + (c * 0.044715) * (x * x)))
    return x + x * t


def _layer_norm(y, g, b):
    mu = jnp.mean(y, axis=-1, keepdims=True)
    yc = y - mu
    var = jnp.mean(yc * yc, axis=-1, keepdims=True)
    return yc * lax.rsqrt(var + LN_EPS) * g + b


def _const_spec(shape):
    zeros = (0,) * len(shape)
    return pl.BlockSpec(shape, lambda *_: zeros, pipeline_mode=pl.Buffered(1))


def _params(sem):
    return pltpu.CompilerParams(dimension_semantics=sem, vmem_limit_bytes=VMEM_LIMIT)


def _lru_kernel(x_ref, win_ref, bin_ref, cw_ref, cb_ref, wg_ref, bg_ref, ap_ref,
                wout_ref, bout_ref, g_ref, b_ref, o_ref,
                xp_buf, x_tail, a_buf, u_buf, h_buf, h_carry):
    tm = x_ref.shape[0]
    W = LRU_WIDTH
    s = pl.program_id(1)

    @pl.when(s == 0)
    def _():
        x_tail[...] = jnp.zeros(x_tail.shape, _F32)
        h_carry[...] = jnp.zeros((SUBLANES, W), _F32)

    hm = tm // 2
    nt = 2 * LRU_BLOCK
    half_c_sp = (-0.5 * LRU_C) * jnp.logaddexp(-ap_ref[...], 0.0)
    sub = lax.broadcasted_iota(jnp.int32, (SUBLANES, W), 0)
    row = lax.broadcasted_iota(jnp.int32, (hm, LRU_BLOCK), 0)

    x_halves = [x_ref[k * hm:(k + 1) * hm, :].astype(_BF16) for k in range(2)]

    def in_proj_tile(k, t):
        return _dot(x_halves[k], win_ref[:, t * nt:(t + 1) * nt]) + bin_ref[:, t * nt:(t + 1) * nt]

    seg = hm // SUBLANES
    per_seg = seg // SUBLANES

    def permuted_start(g):
        return SUBLANES * SUBLANES * (g % per_seg) + g // per_seg

    def recurrence(k, xr_tiles, h_prev, side_jobs):
        results = []
        first_rows = lax.broadcasted_iota(jnp.int32, (SUBLANES, LRU_BLOCK), 0) < 1
        for n in range(LRU_HEADS):
            results += [job() for job in side_jobs[n]]
            cols = slice(n * LRU_BLOCK, (n + 1) * LRU_BLOCK)
            tile = xr_tiles[n // 2]
            xr = tile[:, (n % 2) * LRU_BLOCK:(n % 2 + 1) * LRU_BLOCK]
            for g in range(hm // SUBLANES):
                xp_buf[n, pl.ds(permuted_start(g), SUBLANES, stride=SUBLANES), :] = (
                    xr[g * SUBLANES:(g + 1) * SUBLANES, :])
            xp = xp_buf[n]
            prev = x_tail[n]
            keep = (LRU_CONV - 1) * SUBLANES
            x_tail[n] = xp[hm - keep:, :]
            wrap = [jnp.where(first_rows,
                              pltpu.roll(prev[d * SUBLANES:(d + 1) * SUBLANES, :], 1, 0),
                              pltpu.roll(xp[hm - keep + d * SUBLANES:hm - keep + (d + 1) * SUBLANES, :], 1, 0))
                    for d in range(LRU_CONV - 1)]
            xn = cb_ref[:, cols] + cw_ref[LRU_CONV - 1:LRU_CONV, cols] * xp
            for d in range(1, LRU_CONV):
                back = jnp.concatenate(wrap[LRU_CONV - 1 - d:] + [xp[:hm - d * SUBLANES, :]], axis=0)
                xn = xn + cw_ref[LRU_CONV - 1 - d:LRU_CONV - d, cols] * back
            t = jnp.tanh(_dot(xn.astype(_BF16), wg_ref[n]) + bg_ref[n])
            gate_i = 0.5 + 0.5 * t[:, :LRU_BLOCK]
            a = jnp.exp(half_c_sp[:, cols] + half_c_sp[:, cols] * t[:, LRU_BLOCK:])
            m = jnp.maximum(1.0 - a * a, 0.0)
            mult = jnp.where(m > 0.0, m * lax.rsqrt(m), 0.0)
            if k == 0:
                mult = jnp.where(jnp.logical_and(row == 0, s == 0), 1.0, mult)
            a_buf[k * hm:(k + 1) * hm, cols] = a
            u_buf[k * hm:(k + 1) * hm, cols] = mult * gate_i * xn

        def load(v):
            r = pl.multiple_of(k * hm + v * SUBLANES, SUBLANES)
            return a_buf[pl.ds(r, SUBLANES), :], u_buf[pl.ds(r, SUBLANES), :], r

        def pass1(v, state):
            h, p = state
            a, u, _ = load(v)
            return a * h + u, p * a

        end, prod = lax.fori_loop(0, seg, pass1, (jnp.zeros((SUBLANES, W), _F32),
                                                  jnp.ones((SUBLANES, W), _F32)))
        for d in (1, 2, 4):
            ok = sub >= d
            end = end + prod * jnp.where(ok, pltpu.roll(end, d, 0), 0.0)
            prod = prod * jnp.where(ok, pltpu.roll(prod, d, 0), 1.0)
        seg_out = end + prod * h_prev
        seg_in = jnp.where(sub < 1, h_prev, pltpu.roll(seg_out, 1, 0))

        def pass2(v, h):
            a, u, r = load(v)
            h = a * h + u
            for n in range(LRU_HEADS):
                h_buf[n, pl.ds(r, SUBLANES), :] = h[:, n * LRU_BLOCK:(n + 1) * LRU_BLOCK]
            return h

        lax.fori_loop(0, seg, pass2, seg_in)
        return jnp.broadcast_to(seg_out[SUBLANES - 1:SUBLANES, :], (SUBLANES, W)), results

    def gated(k, y_tiles):
        y_br = _gelu_x2(jnp.concatenate(y_tiles, axis=1))
        h = jnp.concatenate(
            [jnp.concatenate(
                [h_buf[n, pl.ds(k * hm + permuted_start(g), SUBLANES, stride=SUBLANES), :]
                 for g in range(hm // SUBLANES)], axis=0)
             for n in range(LRU_HEADS)], axis=1)
        return (h * y_br).astype(_BF16)

    def out_proj_tile(hy, t):
        return _dot(hy, wout_ref[:, t * nt:(t + 1) * nt]) + bout_ref[:, t * nt:(t + 1) * nt]

    def finish(k, mix_tiles):
        rows = slice(k * hm, (k + 1) * hm)
        mix = jnp.concatenate(mix_tiles, axis=1)
        o_ref[rows, :] = _layer_norm(DN_ALPHA * x_ref[rows, :] + mix, g_ref[...], b_ref[...])

    half = W // nt
    job = functools.partial
    xr0 = [in_proj_tile(0, half + t) for t in range(half)]
    jobs = ([[job(in_proj_tile, 0, t), job(in_proj_tile, 1, half + t)] for t in range(half)]
            + [[job(in_proj_tile, 1, t)] for t in range(half)])
    h_mid, res = recurrence(0, xr0, h_carry[...], jobs)
    y0, xr1, y1 = res[0:2 * half:2], res[1:2 * half:2], res[2 * half:]
    hy0 = gated(0, y0)
    n_out = D_MODEL // nt
    jobs = [[job(out_proj_tile, hy0, t)] for t in range(n_out)] + [[]] * (LRU_HEADS - n_out)
    h_last, mix0 = recurrence(1, xr1, h_mid, jobs)
    h_carry[...] = h_last
    finish(0, mix0)
    hy1 = gated(1, y1)
    finish(1, [out_proj_tile(hy1, t) for t in range(n_out)])


def _lru_layer(x2d, batch, seq, win, b_in, cw, cb, wg, bg, ap, wout, bout, g, b):
    tm = LRU_TOKEN_BLOCK
    W = LRU_WIDTH
    ns = seq // tm
    tok = pl.BlockSpec((tm, D_MODEL), lambda bi, si: (bi * ns + si, 0))
    return pl.pallas_call(
        _lru_kernel,
        grid=(batch, ns),
        in_specs=[tok, _const_spec(win.shape), _const_spec(b_in.shape), _const_spec(cw.shape),
                  _const_spec(cb.shape), _const_spec(wg.shape), _const_spec(bg.shape),
                  _const_spec(ap.shape), _const_spec(wout.shape), _const_spec(bout.shape),
                  _const_spec(g.shape), _const_spec(b.shape)],
        out_specs=tok,
        out_shape=jax.ShapeDtypeStruct(x2d.shape, _F32),
        scratch_shapes=[pltpu.VMEM((LRU_HEADS, tm // 2, LRU_BLOCK), _F32),
                        pltpu.VMEM((LRU_HEADS, (LRU_CONV - 1) * SUBLANES, LRU_BLOCK), _F32),
                        pltpu.VMEM((tm, W), _F32), pltpu.VMEM((tm, W), _F32),
                        pltpu.VMEM((LRU_HEADS, tm, LRU_BLOCK), _F32),
                        pltpu.VMEM((SUBLANES, W), _F32)],
        compiler_params=_params(("arbitrary", "arbitrary")),
        name="lru_layer",
    )(x2d, win, b_in, cw, cb, wg, bg, ap, wout, bout, g, b)


def _ffn_kernel(x_ref, wup_ref, wdn_ref, cw_ref, cb_ref, g_ref, b_ref, o_ref,
                wup_bf, wdn_bf, carry, relay, *, n_cast, blocks_per_seq):
    tm = x_ref.shape[0]
    fc = FF_CHUNK
    s = pl.program_id(0)
    n_groups = tm // SUBLANES
    lane_tiles = D_MODEL // LANES
    taps_back = FFN_CONV - 1

    @pl.when(s < n_cast)
    def _():
        wup_bf[s] = wup_ref[...].astype(_BF16)

    @pl.when(s < D_FF // fc)
    def _():
        wdn_bf[s] = wdn_ref[...].astype(_BF16)

    @pl.when(jnp.logical_and(s >= n_cast, (s - n_cast) % blocks_per_seq == 0))
    def _():
        carry[...] = jnp.zeros(carry.shape, _F32)

    @pl.when(s >= n_cast)
    def _():
        per_seg = n_groups // SUBLANES

        def permuted_start(g):
            return SUBLANES * SUBLANES * (g % per_seg) + g // per_seg

        x_nat = x_ref[...]
        for n in range(lane_tiles):
            for g in range(n_groups):
                relay[n, pl.ds(permuted_start(g), SUBLANES, stride=SUBLANES), :] = (
                    x_nat[g * SUBLANES:(g + 1) * SUBLANES, n * LANES:(n + 1) * LANES])
        x = jnp.concatenate([relay[n] for n in range(lane_tiles)], axis=1)
        xb = x.astype(_BF16)

        def up(c0, width):
            return (_dot(xb, wup_bf[c0 // fc]),
                    _dot(xb, wup_bf[(D_FF + c0) // fc]))

        def conv(h, col0):
            cols = slice(col0, col0 + h.shape[1])
            first_rows = lax.broadcasted_iota(jnp.int32, (SUBLANES, h.shape[1]), 0) < 1
            keep = taps_back * SUBLANES
            prev = carry[:, cols]
            carry[:, cols] = h[tm - keep:, :]
            wrap = [jnp.where(first_rows,
                              pltpu.roll(prev[d * SUBLANES:(d + 1) * SUBLANES, :], 1, 0),
                              pltpu.roll(h[tm - keep + d * SUBLANES:tm - keep + (d + 1) * SUBLANES, :], 1, 0))
                    for d in range(taps_back)]
            out = cb_ref[:, cols] + cw_ref[taps_back:taps_back + 1, cols] * h
            for d in range(1, FFN_CONV):
                back = jnp.concatenate(wrap[taps_back - d:] + [h[:tm - d * SUBLANES, :]], axis=0)
                out = out + cw_ref[taps_back - d:taps_back - d + 1, cols] * back
            return out

        widths = [fc] * (D_FF // fc)
        starts = [sum(widths[:j]) for j in range(len(widths))]
        acc = jnp.zeros((tm, D_MODEL), _F32)
        h_gate, h_lin = up(starts[0], widths[0])
        act = None
        for j in range(len(widths)):
            if j + 1 < len(widths):
                next_gate, next_lin = up(starts[j + 1], widths[j + 1])
            if act is not None:
                acc = acc + _dot(act, wdn_bf[j - 1])
            act = (_gelu_x2(conv(h_gate, starts[j])) * conv(h_lin, D_FF + starts[j])).astype(_BF16)
            h_gate, h_lin = next_gate, next_lin
        acc = acc + _dot(act, wdn_bf[len(widths) - 1])
        y = _layer_norm(DN_ALPHA * x + acc, g_ref[...], b_ref[...])
        for n in range(lane_tiles):
            relay[n] = y[:, n * LANES:(n + 1) * LANES]
        o_ref[...] = jnp.concatenate(
            [jnp.concatenate([relay[n, pl.ds(permuted_start(g), SUBLANES, stride=SUBLANES), :]
                              for g in range(n_groups)], axis=0)
             for n in range(lane_tiles)], axis=1)


def _ffn_layer(x2d, batch, seq, layer, wup_all, cw, cb, wdn_all, g, b):
    tm = TOKEN_BLOCK
    fc = FF_CHUNK
    blocks_per_seq = seq // tm
    n_blocks = batch * blocks_per_seq
    n_up, n_dn = 2 * D_FF // fc, D_FF // fc
    tok = pl.BlockSpec((tm, D_MODEL), lambda s: (jnp.maximum(s - n_up, 0), 0))
    body = functools.partial(_ffn_kernel, n_cast=n_up, blocks_per_seq=blocks_per_seq)
    return pl.pallas_call(
        body,
        grid=(n_up + n_blocks,),
        in_specs=[tok,
                  pl.BlockSpec((None, D_MODEL, fc), lambda s: (layer, 0, jnp.minimum(s, n_up - 1))),
                  pl.BlockSpec((None, fc, D_MODEL), lambda s: (layer, jnp.minimum(s, n_dn - 1), 0)),
                  _const_spec(cw.shape), _const_spec(cb.shape),
                  _const_spec(g.shape), _const_spec(b.shape)],
        out_specs=tok,
        out_shape=jax.ShapeDtypeStruct(x2d.shape, _F32),
        scratch_shapes=[pltpu.VMEM((n_up, D_MODEL, fc), _BF16),
                        pltpu.VMEM((n_dn, fc, D_MODEL), _BF16),
                        pltpu.VMEM(((FFN_CONV - 1) * SUBLANES, 2 * D_FF), _F32),
                        pltpu.VMEM((D_MODEL // LANES, tm, LANES), _F32)],
        compiler_params=_params(("arbitrary",)),
        name="conv_ffn",
    )(x2d, wup_all, wdn_all, cw, cb, g, b)


def _qkv_kernel(x_ref, w_ref, o_ref):
    res = _dot(x_ref[...].astype(_BF16), w_ref[...])
    for n in range(3 * N_HEADS):
        part = res[:, n * HEAD_DIM:(n + 1) * HEAD_DIM]
        if n >= 2 * N_HEADS:
            part = part * (HEAD_DIM ** -0.5)
        o_ref[n] = part.astype(_BF16)


def _qkv_proj(x2d, w):
    tm = LRU_TOKEN_BLOCK
    tokens = x2d.shape[0]
    return pl.pallas_call(
        _qkv_kernel,
        grid=(tokens // tm,),
        in_specs=[pl.BlockSpec((tm, D_MODEL), lambda i: (i, 0)), _const_spec(w.shape)],
        out_specs=pl.BlockSpec((3 * N_HEADS, tm, HEAD_DIM), lambda i: (0, i, 0)),
        out_shape=jax.ShapeDtypeStruct((3 * N_HEADS, tokens, HEAD_DIM), _BF16),
        compiler_params=_params(("arbitrary",)),
        name="kvq_proj",
    )(x2d, w)


def _attn_kernel(q_ref, k_ref, v_ref, x_ref, wo_ref, g_ref, b_ref, y_ref, o_acc, tail_acc, o_done,
                 *, n_blocks):
    tq, tk = ATTN_Q_BLOCK, ATTN_K_BLOCK
    i = pl.program_id(1)
    row = lax.broadcasted_iota(jnp.int32, (tq, tk), 0)
    col = lax.broadcasted_iota(jnp.int32, (tq, tk), 1)
    later = (row > col).astype(_BF16)
    causal = col < row

    def logits(h, j):
        k = k_ref[h, pl.ds(pl.multiple_of(j * tk, tk), tk), :]
        z = lax.dot_general(q_ref[h], k, (((1,), (1,)), ((), ())), preferred_element_type=_F32)
        log_beta = jnp.minimum(z, 0.0) - jnp.log(1.0 + jnp.exp(-jnp.abs(z)))
        return log_beta, log_beta - z

    def values(h, j):
        return v_ref[h, pl.ds(pl.multiple_of(j * tk, tk), tk), :]

    def earlier(h, j, o, tail):
        log_beta, log_1m = logits(h, j)
        suffix = _dot(log_1m.astype(_BF16), later) + tail
        w = jnp.exp(log_beta + suffix)
        return (o + _dot(w.astype(_BF16), values(h, j)),
                tail + jnp.sum(log_1m, axis=1, keepdims=True))

    def stage_logits(h, j, on_diagonal):
        log_beta, log_1m = logits(h, j)
        if on_diagonal:
            log_1m = jnp.where(causal, log_1m, 0.0)
        return log_beta, log_1m.astype(_BF16), jnp.sum(log_1m, axis=1, keepdims=True)

    def stage_weights(h, j, on_diagonal, log_beta, suffix, tail):
        x = log_beta + suffix if tail is None else log_beta + suffix + tail
        w = jnp.exp(x)
        if on_diagonal:
            w = jnp.where(causal, w, 0.0)
        return _dot(w.astype(_BF16), values(h, j))

    def run_tiles(tiles, side_jobs=()):
        first, suffix, out, side = {}, {}, {}, []
        n_iter = len(tiles) + 2
        for t in range(n_iter):
            side += [job() for job in
                     side_jobs[t * len(side_jobs) // n_iter:(t + 1) * len(side_jobs) // n_iter]]
            if t < len(tiles):
                h, j, diag, _ = tiles[t]
                first[t] = stage_logits(h, j, diag)
            if 0 <= t - 1 < len(tiles):
                suffix[t - 1] = _dot(first[t - 1][1], later)
            if 0 <= t - 2 < len(tiles):
                h, j, diag, prev = tiles[t - 2]
                tail = None if prev is None else first[prev][2]
                out[t - 2] = stage_weights(h, j, diag, first[t - 2][0], suffix[t - 2], tail)
        return [out[t] for t in range(len(tiles))], [first[t][2] for t in range(len(tiles))], side

    def out_proj_jobs(o):
        def tile(t):
            return _dot(o, wo_ref[:, t * tk:(t + 1) * tk])
        return [functools.partial(tile, t) for t in range(D_MODEL // tk)]

    def finish_previous(mix_tiles):
        mix = jnp.concatenate(mix_tiles, axis=1)
        y_ref[...] = _layer_norm(DN_ALPHA * x_ref[...] + mix, g_ref[...], b_ref[...])

    def park_outputs():
        o_done[...] = jnp.concatenate([o_acc[h] for h in range(N_HEADS)], axis=1).astype(_BF16)

    def alive(tail):
        return jnp.max(tail) >= EXP_ZERO_CUTOFF

    @pl.when(i == 0)
    def _():
        outs, _, _ = run_tiles([(h, i, True, None) for h in range(N_HEADS)])
        for h in range(N_HEADS):
            o_acc[h] = outs[h]
        park_outputs()

    @pl.when(jnp.logical_and(i > 0, i < n_blocks))
    def _():
        tiles = []
        for h in range(N_HEADS):
            tiles += [(h, i, True, None), (h, i - 1, False, 2 * h)]
        outs, sums, mix = run_tiles(tiles, out_proj_jobs(o_done[...]))
        finish_previous(mix)
        worst = jnp.full((tq, 1), -jnp.inf, _F32)
        for h in range(N_HEADS):
            tail = sums[2 * h] + sums[2 * h + 1]
            o_acc[h] = outs[2 * h] + outs[2 * h + 1]
            tail_acc[h] = tail
            worst = jnp.maximum(worst, tail)

        @pl.when(jnp.logical_and(i > 1, alive(worst)))
        def _():
            def head(h, _):
                def cond(state):
                    j, _, _, live = state
                    return jnp.logical_and(j >= 0, live > 0)

                def body(state):
                    j, o, tail, _ = state
                    o, tail = earlier(h, j, o, tail)
                    return j - 1, o, tail, alive(tail).astype(jnp.int32)

                tail = tail_acc[h]
                state = (i - 2, o_acc[h], tail, alive(tail).astype(jnp.int32))
                o_acc[h] = lax.while_loop(cond, body, state)[1]
                return 0

            lax.fori_loop(0, N_HEADS, head, 0)

        park_outputs()

    @pl.when(i == n_blocks)
    def _():
        finish_previous([job() for job in out_proj_jobs(o_done[...])])


def _attention_layer(kvq, x2d, batch, seq, wo, g, b):
    tq = ATTN_Q_BLOCK
    nq = seq // tq
    tok_prev = pl.BlockSpec((tq, D_MODEL), lambda bi, i: (bi * nq + jnp.maximum(i - 1, 0), 0))

    def kv_batch(bi, i):
        return jnp.minimum(bi + i // nq, batch - 1)

    return pl.pallas_call(
        functools.partial(_attn_kernel, n_blocks=nq),
        grid=(batch, nq + 1),
        in_specs=[pl.BlockSpec((N_HEADS, tq, HEAD_DIM),
                               lambda bi, i: (2, bi * nq + jnp.minimum(i, nq - 1), 0)),
                  pl.BlockSpec((N_HEADS, seq, HEAD_DIM), lambda bi, i: (0, kv_batch(bi, i), 0)),
                  pl.BlockSpec((N_HEADS, seq, HEAD_DIM), lambda bi, i: (1, kv_batch(bi, i), 0)),
                  tok_prev, _const_spec(wo.shape), _const_spec(g.shape), _const_spec(b.shape)],
        out_specs=tok_prev,
        out_shape=jax.ShapeDtypeStruct(x2d.shape, _F32),
        scratch_shapes=[pltpu.VMEM((N_HEADS, tq, HEAD_DIM), _F32),
                        pltpu.VMEM((N_HEADS, tq, 1), _F32),
                        pltpu.VMEM((tq, N_HEADS * HEAD_DIM), _BF16)],
        compiler_params=_params(("arbitrary", "arbitrary")),
        name="stickbreak_attn",
    )(kvq, kvq, kvq, x2d, wo, g, b)


def kernel(x, lru_w_in, lru_b_in, lru_conv_w, lru_conv_b, lru_w_gates, lru_b_gates, lru_a_param, lru_w_out, lru_b_out, kv_w, attn_w_q, attn_w_out, ffn_w_up, ffn_conv_w, ffn_conv_b, ffn_w_down, ln_g, ln_b):
    batch, seq, d = x.shape
    assert d == D_MODEL and seq % TOKEN_BLOCK == 0 and seq % ATTN_Q_BLOCK == 0
    assert lru_w_in.shape[0] == 1 and attn_w_q.shape[0] == 1 and ffn_w_up.shape[0] == DEPTH
    row = lambda v: v.reshape(1, -1)
    h = x.reshape(batch * seq, d)

    half_lin = jnp.concatenate([jnp.ones((D_FF,), _F32), jnp.full((D_FF,), 0.5, _F32)])

    def ffn(h, layer):
        return _ffn_layer(h, batch, seq, layer, ffn_w_up, ffn_conv_w[layer] * half_lin,
                          row(ffn_conv_b[layer] * half_lin), ffn_w_down,
                          row(ln_g[layer, 1]), row(ln_b[layer, 1]))

    h = _lru_layer(h, batch, seq, lru_w_in[0].astype(_BF16), row(lru_b_in[0]), lru_conv_w[0],
                   row(lru_conv_b[0]), (0.5 * lru_w_gates[0]).astype(_BF16),
                   (0.5 * lru_b_gates[0]).reshape(LRU_HEADS, 1, 2 * LRU_BLOCK), row(lru_a_param[0]),
                   (0.5 * lru_w_out[0]).astype(_BF16), row(lru_b_out[0]), row(ln_g[0, 0]), row(ln_b[0, 0]))
    h = ffn(h, 0)
    kvq = _qkv_proj(h, jnp.concatenate([kv_w, attn_w_q[0]], axis=1).astype(_BF16))
    h = _attention_layer(kvq, h, batch, seq, attn_w_out[0].astype(_BF16),
                         row(ln_g[1, 0]), row(ln_b[1, 0]))
    h = ffn(h, 1)
    return h.reshape(batch, seq, d)
```
